```python
import math
import jax, jax.numpy as jnp
from jax import lax
import numpy as np

D_MODEL = 2048
BATCH = 2
SEQ = 16384
DEPTH = 1

HEAD_DIM = 128
A_Q_HEADS = 8
A_KV_HEADS = 2
A_REP = A_Q_HEADS // A_KV_HEADS
WINDOW = 128
BLOCK = 128
B_HEADS = 4
D_FF = 5632
ROPE_THETA = 10000.0
EPS = 1e-6
NEG_INF = -1e30

A_Q = A_Q_HEADS * HEAD_DIM
A_KV = A_KV_HEADS * HEAD_DIM
A_WIDTH = A_Q
B_QK = B_HEADS * 2 * HEAD_DIM
B_V = B_HEADS * 2 * HEAD_DIM
B_WIDTH = B_V
GATE_W = 2 * D_MODEL
W_IN_COLS = A_Q + 2 * A_KV + 2 * B_QK + B_V + GATE_W
SPLITS = tuple(np.cumsum([A_Q, A_KV, A_KV, B_QK, B_QK, B_V, D_MODEL])[:].tolist())

kernel_name = "hybrid_gated_window_gqa_diff_attn_macaron"


def rmsnorm(x, g):
    xf = x.astype(jnp.float32)
    y = xf * lax.rsqrt(jnp.mean(xf * xf, axis=-1, keepdims=True) + EPS)
    return (y * g.astype(jnp.float32)).astype(x.dtype)


def swiglu(h, w_gate, w_up, w_down):
    return (jax.nn.silu(h @ w_gate) * (h @ w_up)) @ w_down


def rope_tables(seq, dim):
    pos = jnp.arange(seq, dtype=jnp.float32)
    inv = ROPE_THETA ** (-jnp.arange(0, dim, 2, dtype=jnp.float32) / dim)
    ang = pos[:, None] * inv[None, :]
    return jnp.cos(ang), jnp.sin(ang)


def apply_rope(x, cos, sin):
    shape = (1, x.shape[1]) + (1,) * (x.ndim - 3) + (x.shape[-1] // 2,)
    c, s = cos.reshape(shape), sin.reshape(shape)
    xf = x.astype(jnp.float32)
    x1, x2 = jnp.split(xf, 2, axis=-1)
    return jnp.concatenate([x1 * c - x2 * s, x2 * c + x1 * s], axis=-1).astype(x.dtype)


def windowed_gqa_sink(q, k, v, sink):
    b, s = q.shape[0], q.shape[1]
    nb = s // BLOCK
    qb = q.reshape(b, nb, BLOCK, A_KV_HEADS, A_REP, HEAD_DIM)

    def band(t):
        tp = jnp.pad(t, ((0, 0), (BLOCK, BLOCK), (0, 0), (0, 0)))
        tb = tp.reshape(b, nb + 2, BLOCK, A_KV_HEADS, HEAD_DIM)
        return jnp.concatenate([tb[:, :-2], tb[:, 1:-1], tb[:, 2:]], axis=2)

    kband, vband = band(k), band(v)
    scores = jnp.einsum('bnqgrd,bnkgd->bngrqk', qb, kband,
                        preferred_element_type=jnp.float32) * (HEAD_DIM ** -0.5)
    blk = jnp.arange(nb)[:, None, None] * BLOCK
    qpos = blk + jnp.arange(BLOCK)[None, :, None]
    kpos = blk + jnp.arange(3 * BLOCK)[None, None, :] - BLOCK
    valid = (jnp.abs(kpos - qpos) <= WINDOW) & (kpos >= 0) & (kpos < s)
    scores = jnp.where(valid[None, :, None, None], scores, NEG_INF)
    sk = sink.astype(jnp.float32).reshape(1, 1, A_KV_HEADS, A_REP, 1, 1)
    m = jnp.maximum(jnp.max(scores, axis=-1, keepdims=True), sk)
    e = jnp.exp(scores - m)
    p = e / (jnp.sum(e, axis=-1, keepdims=True) + jnp.exp(sk - m))
    out = jnp.einsum('bngrqk,bnkgd->bnqgrd', p.astype(v.dtype), vband)
    return out.reshape(b, s, A_WIDTH)


def differential_attention(q, k, v, lam, subln_g, lam_init):
    b, s = q.shape[0], q.shape[1]
    nb = s // BLOCK
    qblocks = jnp.moveaxis(q.reshape(b, nb, BLOCK, B_HEADS, 2, HEAD_DIM), 1, 0)

    def one_block(qblk):
        sc = jnp.einsum('bqhcd,bkhcd->bhcqk', qblk, k,
                        preferred_element_type=jnp.float32) * (HEAD_DIM ** -0.5)
        p = jax.nn.softmax(sc, axis=-1)
        attn = p[:, :, 0] - lam * p[:, :, 1]
        return jnp.einsum('bhqk,bkhe->bqhe', attn.astype(v.dtype), v)

    out = lax.map(one_block, qblocks)
    out = jnp.moveaxis(out, 0, 1).reshape(b, s, B_HEADS, 2 * HEAD_DIM)
    out = rmsnorm(out, subln_g) * (1.0 - lam_init)
    return out.reshape(b, s, B_WIDTH)


def setup_inputs(seed: int = 0) -> dict:
    key = jax.random.key(seed)
    ks = jax.random.split(key, 24)
    L = DEPTH

    def nrm(k, shape, fan_in):
        return jax.random.normal(k, shape, jnp.float32) * (fan_in ** -0.5)

    def gain(k, dim):
        return 1.0 + 0.05 * jax.random.normal(k, (L, dim), jnp.float32)

    return {
        "x": jax.random.normal(ks[0], (BATCH, SEQ, D_MODEL), jnp.float32),
        "ffn1_pre_g": gain(ks[1], D_MODEL),
        "ffn1_w_gate": nrm(ks[2], (L, D_MODEL, D_FF), D_MODEL),
        "ffn1_w_up": nrm(ks[3], (L, D_MODEL, D_FF), D_MODEL),
        "ffn1_w_down": nrm(ks[4], (L, D_FF, D_MODEL), D_FF),
        "ffn1_post_g": gain(ks[5], D_MODEL),
        "mix_pre_g": gain(ks[6], D_MODEL),
        "w_in": nrm(ks[7], (L, D_MODEL, W_IN_COLS), D_MODEL),
        "gate_bias": 0.01 * jax.random.normal(ks[8], (L, GATE_W), jnp.float32),
        "sink_logit": 0.5 * jax.random.normal(ks[9], (L, A_Q_HEADS), jnp.float32),
        "lambda_q1": 0.1 * jax.random.normal(ks[10], (L, HEAD_DIM), jnp.float32),
        "lambda_k1": 0.1 * jax.random.normal(ks[11], (L, HEAD_DIM), jnp.float32),
        "lambda_q2": 0.1 * jax.random.normal(ks[12], (L, HEAD_DIM), jnp.float32),
        "lambda_k2": 0.1 * jax.random.normal(ks[13], (L, HEAD_DIM), jnp.float32),
        "subln_g": gain(ks[14], 2 * HEAD_DIM),
        "w_proj_a": nrm(ks[15], (L, A_WIDTH, D_MODEL), A_WIDTH),
        "w_proj_b": nrm(ks[16], (L, B_WIDTH, D_MODEL), B_WIDTH),
        "w_out": nrm(ks[17], (L, D_MODEL, D_MODEL), D_MODEL),
        "mix_post_g": gain(ks[18], D_MODEL),
        "ffn2_pre_g": gain(ks[19], D_MODEL),
        "ffn2_w_gate": nrm(ks[20], (L, D_MODEL, D_FF), D_MODEL),
        "ffn2_w_up": nrm(ks[21], (L, D_MODEL, D_FF), D_MODEL),
        "ffn2_w_down": nrm(ks[22], (L, D_FF, D_MODEL), D_FF),
        "ffn2_post_g": gain(ks[23], D_MODEL),
    }


def reference(x, ffn1_pre_g, ffn1_w_gate, ffn1_w_up, ffn1_w_down, ffn1_post_g,
              mix_pre_g, w_in, gate_bias, sink_logit, lambda_q1, lambda_k1, lambda_q2,
              lambda_k2, subln_g, w_proj_a, w_proj_b, w_out, mix_post_g,
              ffn2_pre_g, ffn2_w_gate, ffn2_w_up, ffn2_w_down, ffn2_post_g):
    b, s = x.shape[0], x.shape[1]
    cos, sin = rope_tables(s, HEAD_DIM)
    for l in range(DEPTH):
        lam_init = 0.8 - 0.6 * math.exp(-0.3 * l)
        f = swiglu(rmsnorm(x, ffn1_pre_g[l]), ffn1_w_gate[l], ffn1_w_up[l], ffn1_w_down[l])
        x = x + 0.5 * rmsnorm(f, ffn1_post_g[l])

        h = rmsnorm(x, mix_pre_g[l])
        proj = h @ w_in[l]
        qa, ka, va, qb, kb, vb, ga, gb = jnp.split(proj, SPLITS, axis=-1)
        ga = jax.nn.sigmoid(ga + gate_bias[l, :D_MODEL])
        gb = jax.nn.sigmoid(gb + gate_bias[l, D_MODEL:])

        qa = apply_rope(qa.reshape(b, s, A_Q_HEADS, HEAD_DIM), cos, sin)
        ka = apply_rope(ka.reshape(b, s, A_KV_HEADS, HEAD_DIM), cos, sin)
        va = va.reshape(b, s, A_KV_HEADS, HEAD_DIM)
        out_a = windowed_gqa_sink(qa, ka, va, sink_logit[l])

        qb = apply_rope(qb.reshape(b, s, B_HEADS, 2, HEAD_DIM), cos, sin)
        kb = apply_rope(kb.reshape(b, s, B_HEADS, 2, HEAD_DIM), cos, sin)
        vb = vb.reshape(b, s, B_HEADS, 2 * HEAD_DIM)
        lam = (jnp.exp(jnp.sum(lambda_q1[l].astype(jnp.float32) * lambda_k1[l].astype(jnp.float32)))
               - jnp.exp(jnp.sum(lambda_q2[l].astype(jnp.float32) * lambda_k2[l].astype(jnp.float32)))
               + lam_init)
        out_b = differential_attention(qb, kb, vb, lam, subln_g[l], lam_init)

        merged = ga * (out_a @ w_proj_a[l]) + gb * (out_b @ w_proj_b[l])
        x = x + rmsnorm(merged @ w_out[l], mix_post_g[l])

        f = swiglu(rmsnorm(x, ffn2_pre_g[l]), ffn2_w_gate[l], ffn2_w_up[l], ffn2_w_down[l])
        x = x + 0.5 * rmsnorm(f, ffn2_post_g[l])
    return x
```

```python
import functools
import math

import jax
import jax.numpy as jnp
from jax import lax
from jax.experimental import pallas as pl
from jax.experimental.pallas import tpu as pltpu

D_MODEL = 2048
HEAD_DIM = 128
A_Q_HEADS = 8
A_KV_HEADS = 2
A_REP = A_Q_HEADS // A_KV_HEADS
WINDOW = 128
B_HEADS = 4
D_FF = 5632
ROPE_THETA = 10000.0
EPS = 1e-6
NEG_INF = -1e30
LOG2E = 1.4426950408889634
QK_SCALE = HEAD_DIM ** -0.5 * LOG2E

A_Q = A_Q_HEADS * HEAD_DIM
A_KV = A_KV_HEADS * HEAD_DIM
B_QK = B_HEADS * 2 * HEAD_DIM
B_V = B_HEADS * 2 * HEAD_DIM
ROPE_COLS = A_Q + A_KV + 2 * B_QK
V_COLS = A_KV + B_V
GATE_COLS = 2 * D_MODEL

LANES = 128
MXU_N = 256
VMEM_LIMIT = 56 * 1024 * 1024

F32 = jnp.float32
BF16 = jnp.bfloat16


def _rms(x, g):
    ms = jnp.mean(x * x, axis=-1, keepdims=True)
    return x * lax.rsqrt(ms + EPS) * g


def _resident(shape):
    return pl.BlockSpec(shape, lambda *_: (0,) * len(shape), pipeline_mode=pl.Buffered(1))


def _ffn_body(*refs, nf, emit_next):
    if emit_next:
        (x_ref, gpre_ref, wg_ref, wu_ref, wd_ref, gpost_ref, gnext_ref,
         o_ref, hn_ref, h_scr, acc_scr) = refs
    else:
        (x_ref, gpre_ref, wg_ref, wu_ref, wd_ref, gpost_ref,
         o_ref, h_scr, acc_scr) = refs
    f = pl.program_id(1)

    @pl.when(f == 0)
    def _():
        h_scr[...] = _rms(x_ref[...], gpre_ref[...]).astype(BF16)
        acc_scr[...] = jnp.zeros_like(acc_scr)

    h = h_scr[...]
    gate = jnp.dot(h, wg_ref[...], preferred_element_type=F32)
    up = jnp.dot(h, wu_ref[...], preferred_element_type=F32)
    a = gate * (1.0 / (1.0 + jnp.exp(-gate))) * up
    acc_scr[...] += jnp.dot(a.astype(BF16), wd_ref[...], preferred_element_type=F32)

    @pl.when(f == nf - 1)
    def _():
        xo = x_ref[...] + 0.5 * _rms(acc_scr[...], gpost_ref[...])
        o_ref[...] = xo
        if emit_next:
            hn_ref[...] = _rms(xo, gnext_ref[...]).astype(BF16)


def _ffn(x, g_pre, wg, wu, wd, g_post, g_next=None, *, tm=512, tf=512):
    t, d = x.shape
    nf = D_FF // tf
    emit_next = g_next is not None
    row = pl.BlockSpec((tm, d), lambda i, f: (i, 0))
    vec = pl.BlockSpec((1, d), lambda i, f: (0, 0))
    in_specs = [row, vec,
                pl.BlockSpec((d, tf), lambda i, f: (0, f)),
                pl.BlockSpec((d, tf), lambda i, f: (0, f)),
                pl.BlockSpec((tf, d), lambda i, f: (f, 0)),
                vec]
    args = [x, g_pre, wg, wu, wd, g_post]
    out_shape = [jax.ShapeDtypeStruct((t, d), F32)]
    out_specs = [row]
    if emit_next:
        in_specs.append(vec)
        args.append(g_next)
        out_shape.append(jax.ShapeDtypeStruct((t, d), BF16))
        out_specs.append(row)
    res = pl.pallas_call(
        functools.partial(_ffn_body, nf=nf, emit_next=emit_next),
        grid=(t // tm, nf),
        in_specs=in_specs,
        out_specs=out_specs,
        out_shape=out_shape,
        scratch_shapes=[pltpu.VMEM((tm, d), BF16), pltpu.VMEM((tm, d), F32)],
        compiler_params=pltpu.CompilerParams(
            dimension_semantics=("parallel", "arbitrary"), vmem_limit_bytes=VMEM_LIMIT),
        name="ffn_next" if emit_next else "ffn",
    )(*args)
    return res if emit_next else res[0]


def _proj_rope_body(h_ref, w_ref, cos_ref, sin_ref, o_ref, *, head_scales):
    h = h_ref[...]
    cos = cos_ref[...]
    sin = sin_ref[...]
    heads_per_chunk = MXU_N // HEAD_DIM
    for c in range(len(head_scales) // heads_per_chunk):
        acc = jnp.dot(h, w_ref[:, c * MXU_N:(c + 1) * MXU_N], preferred_element_type=F32)
        for k in range(heads_per_chunk):
            hd = c * heads_per_chunk + k
            a = acc[:, k * HEAD_DIM:(k + 1) * HEAD_DIM]
            r = a * cos + pltpu.roll(a, HEAD_DIM // 2, 1) * sin
            if head_scales[hd] != 1.0:
                r = r * head_scales[hd]
            o_ref[:, hd * HEAD_DIM:(hd + 1) * HEAD_DIM] = r.astype(BF16)


def _proj_plain_body(h_ref, w_ref, o_ref, *, chunk):
    h = h_ref[...]
    for c in range(w_ref.shape[1] // chunk):
        acc = jnp.dot(h, w_ref[:, c * chunk:(c + 1) * chunk], preferred_element_type=F32)
        o_ref[:, c * chunk:(c + 1) * chunk] = acc.astype(BF16)


def _proj_gate_body(h_ref, w_ref, b_ref, o_ref, *, chunk):
    h = h_ref[...]
    for c in range(w_ref.shape[1] // chunk):
        acc = jnp.dot(h, w_ref[:, c * chunk:(c + 1) * chunk], preferred_element_type=F32)
        z = acc + b_ref[:, c * chunk:(c + 1) * chunk]
        o_ref[:, c * chunk:(c + 1) * chunk] = (1.0 / (1.0 + jnp.exp(-z))).astype(BF16)


def _proj_call(body, h, w, extra, extra_specs, n_out, name, *, tm):
    t, d = h.shape
    return pl.pallas_call(
        body,
        grid=(t // tm,),
        in_specs=[pl.BlockSpec((tm, d), lambda i: (i, 0)), _resident(w.shape)] + extra_specs,
        out_specs=pl.BlockSpec((tm, n_out), lambda i: (i, 0)),
        out_shape=jax.ShapeDtypeStruct((t, n_out), BF16),
        compiler_params=pltpu.CompilerParams(
            dimension_semantics=("parallel",), vmem_limit_bytes=VMEM_LIMIT),
        name=name,
    )(h, w, *extra)


def _attn_a_body(sink_ref, q_ref, kp_ref, kc_ref, kn_ref, vp_ref, vc_ref, vn_ref, o_ref, *, tq, nq):
    i = pl.program_id(1)
    nk = tq + 2 * WINDOW
    row = lax.broadcasted_iota(jnp.int32, (tq, nk), 0)
    col = lax.broadcasted_iota(jnp.int32, (tq, nk), 1)
    rel = col - WINDOW - row
    valid = (rel >= -WINDOW) & (rel <= WINDOW)
    valid = valid & ((col >= WINDOW) | (i > 0))
    valid = valid & ((col < tq + WINDOW) | (i < nq - 1))
    k_all = jnp.concatenate([kp_ref[0], kc_ref[0], kn_ref[0]], axis=0)
    v_all = jnp.concatenate([vp_ref[0], vc_ref[0], vn_ref[0]], axis=0)
    for g in range(A_KV_HEADS):
        k = k_all[:, g * HEAD_DIM:(g + 1) * HEAD_DIM]
        v = v_all[:, g * HEAD_DIM:(g + 1) * HEAD_DIM]
        for r in range(A_REP):
            hd = g * A_REP + r
            q = q_ref[0, :, hd * HEAD_DIM:(hd + 1) * HEAD_DIM]
            s = lax.dot_general(q, k, (((1,), (1,)), ((), ())), preferred_element_type=F32)
            s = jnp.where(valid, s, NEG_INF)
            sk = sink_ref[hd] * LOG2E
            m = jnp.maximum(jnp.max(s, axis=-1, keepdims=True), sk)
            e = jnp.exp2(s - m)
            den = jnp.sum(e, axis=-1, keepdims=True) + jnp.exp2(sk - m)
            o = jnp.dot(e.astype(BF16), v, preferred_element_type=F32) / den
            o_ref[0, :, hd * HEAD_DIM:(hd + 1) * HEAD_DIM] = o.astype(BF16)


def _attn_a(rope3, v3, sink, *, tq=256):
    b, s, _ = rope3.shape
    nq = s // tq
    nblk = s // WINDOW
    per = tq // WINDOW
    ka_blk = A_Q // A_KV
    prev_map = lambda bi, i: (bi, jnp.maximum(i * per - 1, 0), ka_blk)
    cur_map = lambda bi, i: (bi, i, ka_blk)
    next_map = lambda bi, i: (bi, jnp.minimum((i + 1) * per, nblk - 1), ka_blk)
    vprev_map = lambda bi, i: (bi, jnp.maximum(i * per - 1, 0), 0)
    vcur_map = lambda bi, i: (bi, i, 0)
    vnext_map = lambda bi, i: (bi, jnp.minimum((i + 1) * per, nblk - 1), 0)
    return pl.pallas_call(
        functools.partial(_attn_a_body, tq=tq, nq=nq),
        grid=(b, nq),
        in_specs=[
            pl.BlockSpec(memory_space=pltpu.SMEM),
            pl.BlockSpec((1, tq, A_Q), lambda bi, i: (bi, i, 0)),
            pl.BlockSpec((1, WINDOW, A_KV), prev_map),
            pl.BlockSpec((1, tq, A_KV), cur_map),
            pl.BlockSpec((1, WINDOW, A_KV), next_map),
            pl.BlockSpec((1, WINDOW, A_KV), vprev_map),
            pl.BlockSpec((1, tq, A_KV), vcur_map),
            pl.BlockSpec((1, WINDOW, A_KV), vnext_map),
        ],
        out_specs=pl.BlockSpec((1, tq, A_Q), lambda bi, i: (bi, i, 0)),
        out_shape=jax.ShapeDtypeStruct((b, s, A_Q), BF16),
        compiler_params=pltpu.CompilerParams(
            dimension_semantics=("parallel", "parallel"), vmem_limit_bytes=VMEM_LIMIT),
        name="attn_a",
    )(sink, rope3, rope3, rope3, rope3, v3, v3, v3)


def _attn_b_body(q_ref, k_ref, v_ref, lam_ref, g_ref, o_ref, m_scr, l_scr, acc_scr,
                 *, tq, tk, nk, lam_init):
    m_scr[...] = jnp.full_like(m_scr, NEG_INF)
    l_scr[...] = jnp.zeros_like(l_scr)
    acc_scr[...] = jnp.zeros_like(acc_scr)

    def step(c, carry):
        off = pl.multiple_of(c * tk, tk)
        kc = k_ref[0, pl.ds(off, tk), :]
        vc = v_ref[0, pl.ds(off, tk), :]
        for comp in range(2):
            q = q_ref[0, :, comp * HEAD_DIM:(comp + 1) * HEAD_DIM]
            k = kc[:, comp * HEAD_DIM:(comp + 1) * HEAD_DIM]
            s = lax.dot_general(q, k, (((1,), (1,)), ((), ())), preferred_element_type=F32)
            m_prev = m_scr[comp]
            m_new = jnp.maximum(m_prev, jnp.max(s, axis=-1, keepdims=True))
            alpha = jnp.exp2(m_prev - m_new)
            e = jnp.exp2(s - m_new)
            l_scr[comp] = alpha * l_scr[comp] + jnp.sum(e, axis=-1, keepdims=True)
            acc_scr[comp] = alpha * acc_scr[comp] + jnp.dot(
                e.astype(BF16), vc, preferred_element_type=F32)
            m_scr[comp] = m_new
        return carry

    lax.fori_loop(0, nk, step, 0)

    lam_v = lam_ref[...]
    lam = (jnp.exp(jnp.sum(lam_v[0:1] * lam_v[1:2], axis=-1, keepdims=True))
           - jnp.exp(jnp.sum(lam_v[2:3] * lam_v[3:4], axis=-1, keepdims=True))
           + lam_init)
    o = acc_scr[0] / l_scr[0] - lam * (acc_scr[1] / l_scr[1])
    o_ref[0] = (_rms(o, g_ref[...]) * (1.0 - lam_init)).astype(BF16)


def _attn_b(rope3, v3, lam_vecs, subln_g, lam_init, *, tq=512, tk=512):
    b, s, _ = rope3.shape
    w = 2 * HEAD_DIM
    qb_blk = (A_Q + A_KV) // w
    kb_blk = qb_blk + B_HEADS
    vb_blk = A_KV // w
    return pl.pallas_call(
        functools.partial(_attn_b_body, tq=tq, tk=tk, nk=s // tk, lam_init=lam_init),
        grid=(b, B_HEADS, s // tq),
        in_specs=[
            pl.BlockSpec((1, tq, w), lambda bi, h, i: (bi, i, qb_blk + h)),
            pl.BlockSpec((1, s, w), lambda bi, h, i: (bi, 0, kb_blk + h)),
            pl.BlockSpec((1, s, w), lambda bi, h, i: (bi, 0, vb_blk + h)),
            pl.BlockSpec((4, HEAD_DIM), lambda bi, h, i: (0, 0)),
            pl.BlockSpec((1, w), lambda bi, h, i: (0, 0)),
        ],
        out_specs=pl.BlockSpec((1, tq, w), lambda bi, h, i: (bi, i, h)),
        out_shape=jax.ShapeDtypeStruct((b, s, B_V), BF16),
        scratch_shapes=[pltpu.VMEM((2, tq, 1), F32), pltpu.VMEM((2, tq, 1), F32),
                        pltpu.VMEM((2, tq, w), F32)],
        compiler_params=pltpu.CompilerParams(
            dimension_semantics=("parallel", "parallel", "arbitrary"),
            vmem_limit_bytes=VMEM_LIMIT),
        name="attn_b",
    )(rope3, rope3, v3, lam_vecs, subln_g)


def _merge_body(oa_ref, ob_ref, gate_ref, x_ref, wa_ref, wb_ref, wo_ref, g_ref, o_ref):
    pa = jnp.dot(oa_ref[...], wa_ref[...], preferred_element_type=F32)
    pb = jnp.dot(ob_ref[...], wb_ref[...], preferred_element_type=F32)
    ga = gate_ref[:, :D_MODEL].astype(F32)
    gb = gate_ref[:, D_MODEL:].astype(F32)
    merged = (ga * pa + gb * pb).astype(BF16)
    y = jnp.dot(merged, wo_ref[...], preferred_element_type=F32)
    o_ref[...] = x_ref[...] + _rms(y, g_ref[...])


def _merge(oa, ob, gates, x1, wa, wb, wo, g_post, *, tm=512):
    t, d = x1.shape
    rows = lambda n: pl.BlockSpec((tm, n), lambda i: (i, 0))
    return pl.pallas_call(
        _merge_body,
        grid=(t // tm,),
        in_specs=[rows(A_Q), rows(B_V), rows(GATE_COLS), rows(d),
                  _resident(wa.shape), _resident(wb.shape), _resident(wo.shape),
                  pl.BlockSpec((1, d), lambda i: (0, 0))],
        out_specs=rows(d),
        out_shape=jax.ShapeDtypeStruct((t, d), F32),
        compiler_params=pltpu.CompilerParams(
            dimension_semantics=("parallel",), vmem_limit_bytes=VMEM_LIMIT),
        name="merge",
    )(oa, ob, gates, x1, wa, wb, wo, g_post)


def _rope_tables(seq):
    pos = jnp.arange(seq, dtype=F32)
    inv = ROPE_THETA ** (-jnp.arange(0, HEAD_DIM, 2, dtype=F32) / HEAD_DIM)
    ang = pos[:, None] * inv[None, :]
    cos, sin = jnp.cos(ang), jnp.sin(ang)
    return jnp.concatenate([cos, cos], axis=-1), jnp.concatenate([-sin, sin], axis=-1)


def kernel(x, ffn1_pre_g, ffn1_w_gate, ffn1_w_up, ffn1_w_down, ffn1_post_g, mix_pre_g, w_in, gate_bias, sink_logit, lambda_q1, lambda_k1, lambda_q2, lambda_k2, subln_g, w_proj_a, w_proj_b, w_out, mix_post_g, ffn2_pre_g, ffn2_w_gate, ffn2_w_up, ffn2_w_down, ffn2_post_g):
    b, s, d = x.shape
    t = b * s
    depth = ffn1_pre_g.shape[0]
    cos2, sin2 = _rope_tables(s)
    xt = x.reshape(t, d)
    q_scaled = (QK_SCALE,) * A_Q_HEADS + (1.0,) * A_KV_HEADS + (QK_SCALE,) * (2 * B_HEADS) + (1.0,) * (2 * B_HEADS)
    tm_proj = 512
    for l in range(depth):
        lam_init = 0.8 - 0.6 * math.exp(-0.3 * l)
        row = lambda p: p[l:l + 1]
        bf = lambda p: p[l].astype(BF16)

        xt, h = _ffn(xt, row(ffn1_pre_g), bf(ffn1_w_gate), bf(ffn1_w_up), bf(ffn1_w_down),
                     row(ffn1_post_g), row(mix_pre_g))

        w = w_in[l]
        qa_end, ka_end, va_end = A_Q, A_Q + A_KV, A_Q + 2 * A_KV
        qb_end = va_end + B_QK
        kb_end = qb_end + B_QK
        vb_end = kb_end + B_V
        w_rope = jnp.concatenate([w[:, :ka_end], w[:, va_end:kb_end]], axis=1).astype(BF16)
        w_v = jnp.concatenate([w[:, ka_end:va_end], w[:, kb_end:vb_end]], axis=1).astype(BF16)
        w_gate = w[:, vb_end:].astype(BF16)

        ns = s // tm_proj
        tab = pl.BlockSpec((tm_proj, HEAD_DIM), lambda i: (i % ns, 0))
        rope = _proj_call(functools.partial(_proj_rope_body, head_scales=q_scaled),
                          h, w_rope, [cos2, sin2], [tab, tab], ROPE_COLS, "proj_rope", tm=tm_proj)
        vals = _proj_call(functools.partial(_proj_plain_body, chunk=MXU_N),
                          h, w_v, [], [], V_COLS, "proj_v", tm=tm_proj)
        gates = _proj_call(functools.partial(_proj_gate_body, chunk=2 * MXU_N),
                           h, w_gate, [gate_bias[l:l + 1]], [_resident((1, GATE_COLS))],
                           GATE_COLS, "proj_gate", tm=tm_proj)

        rope3 = rope.reshape(b, s, ROPE_COLS)
        v3 = vals.reshape(b, s, V_COLS)
        out_a = _attn_a(rope3, v3, sink_logit[l])
        lam_vecs = jnp.stack([lambda_q1[l], lambda_k1[l], lambda_q2[l], lambda_k2[l]]).astype(F32)
        out_b = _attn_b(rope3, v3, lam_vecs, row(subln_g), lam_init)

        xt = _merge(out_a.reshape(t, A_Q), out_b.reshape(t, B_V), gates, xt,
                    bf(w_proj_a), bf(w_proj_b), bf(w_out), row(mix_post_g))

        xt = _ffn(xt, row(ffn2_pre_g), bf(ffn2_w_gate), bf(ffn2_w_up), bf(ffn2_w_down),
                  row(ffn2_post_g))
    return xt.reshape(b, s, d)
```

```python
import functools
import math

import jax
import jax.numpy as jnp
from jax import lax
from jax.experimental import pallas as pl
from jax.experimental.pallas import tpu as pltpu

D_MODEL = 2048
HEAD_DIM = 128
A_Q_HEADS = 8
A_KV_HEADS = 2
A_REP = A_Q_HEADS // A_KV_HEADS
WINDOW = 128
B_HEADS = 4
D_FF = 5632
ROPE_THETA = 10000.0
EPS = 1e-6
NEG_INF = -1e30
LOG2E = 1.4426950408889634
QK_SCALE = HEAD_DIM ** -0.5 * LOG2E

A_Q = A_Q_HEADS * HEAD_DIM
A_KV = A_KV_HEADS * HEAD_DIM
B_QK = B_HEADS * 2 * HEAD_DIM
B_V = B_HEADS * 2 * HEAD_DIM
ROPE_COLS = A_Q + A_KV + 2 * B_QK
V_COLS = A_KV + B_V
GATE_COLS = 2 * D_MODEL

LANES = 128
MXU_N = 256
VMEM_LIMIT = 56 * 1024 * 1024

F32 = jnp.float32
BF16 = jnp.bfloat16


def _rms(x, g):
    ms = jnp.mean(x * x, axis=-1, keepdims=True)
    return x * lax.rsqrt(ms + EPS) * g


def _resident(shape):
    return pl.BlockSpec(shape, lambda *_: (0,) * len(shape), pipeline_mode=pl.Buffered(1))


def _ffn_body(*refs, nf, emit_next):
    if emit_next:
        (x_ref, gpre_ref, wg_ref, wu_ref, wd_ref, gpost_ref, gnext_ref,
         o_ref, hn_ref, h_scr, acc_scr) = refs
    else:
        (x_ref, gpre_ref, wg_ref, wu_ref, wd_ref, gpost_ref,
         o_ref, h_scr, acc_scr) = refs
    f = pl.program_id(1)

    @pl.when(f == 0)
    def _():
        h_scr[...] = _rms(x_ref[...], gpre_ref[...]).astype(BF16)
        acc_scr[...] = jnp.zeros_like(acc_scr)

    h = h_scr[...]
    gate = jnp.dot(h, wg_ref[...], preferred_element_type=F32)
    up = jnp.dot(h, wu_ref[...], preferred_element_type=F32)
    a = gate * (1.0 / (1.0 + jnp.exp(-gate))) * up
    acc_scr[...] += jnp.dot(a.astype(BF16), wd_ref[...], preferred_element_type=F32)

    @pl.when(f == nf - 1)
    def _():
        xo = x_ref[...] + 0.5 * _rms(acc_scr[...], gpost_ref[...])
        o_ref[...] = xo
        if emit_next:
            hn_ref[...] = _rms(xo, gnext_ref[...]).astype(BF16)


def _ffn(x, g_pre, wg, wu, wd, g_post, g_next=None, *, tm=512, tf=512):
    t, d = x.shape
    nf = D_FF // tf
    emit_next = g_next is not None
    row = pl.BlockSpec((tm, d), lambda i, f: (i, 0))
    vec = pl.BlockSpec((1, d), lambda i, f: (0, 0))
    in_specs = [row, vec,
                pl.BlockSpec((d, tf), lambda i, f: (0, f)),
                pl.BlockSpec((d, tf), lambda i, f: (0, f)),
                pl.BlockSpec((tf, d), lambda i, f: (f, 0)),
                vec]
    args = [x, g_pre, wg, wu, wd, g_post]
    out_shape = [jax.ShapeDtypeStruct((t, d), F32)]
    out_specs = [row]
    if emit_next:
        in_specs.append(vec)
        args.append(g_next)
        out_shape.append(jax.ShapeDtypeStruct((t, d), BF16))
        out_specs.append(row)
    res = pl.pallas_call(
        functools.partial(_ffn_body, nf=nf, emit_next=emit_next),
        grid=(t // tm, nf),
        in_specs=in_specs,
        out_specs=out_specs,
        out_shape=out_shape,
        scratch_shapes=[pltpu.VMEM((tm, d), BF16), pltpu.VMEM((tm, d), F32)],
        compiler_params=pltpu.CompilerParams(
            dimension_semantics=("parallel", "arbitrary"), vmem_limit_bytes=VMEM_LIMIT),
        name="ffn_next" if emit_next else "ffn",
    )(*args)
    return res if emit_next else res[0]


def _proj_rope_body(h_ref, w_ref, cos_ref, sin_ref, o_ref, *, head_scales):
    h = h_ref[...]
    cos = cos_ref[...]
    sin = sin_ref[...]
    heads_per_chunk = MXU_N // HEAD_DIM
    for c in range(len(head_scales) // heads_per_chunk):
        acc = jnp.dot(h, w_ref[:, c * MXU_N:(c + 1) * MXU_N], preferred_element_type=F32)
        for k in range(heads_per_chunk):
            hd = c * heads_per_chunk + k
            a = acc[:, k * HEAD_DIM:(k + 1) * HEAD_DIM]
            r = a * cos + pltpu.roll(a, HEAD_DIM // 2, 1) * sin
            if head_scales[hd] != 1.0:
                r = r * head_scales[hd]
            o_ref[:, hd * HEAD_DIM:(hd + 1) * HEAD_DIM] = r.astype(BF16)


def _proj_plain_body(h_ref, w_ref, o_ref, *, chunk):
    h = h_ref[...]
    for c in range(w_ref.shape[1] // chunk):
        acc = jnp.dot(h, w_ref[:, c * chunk:(c + 1) * chunk], preferred_element_type=F32)
        o_ref[:, c * chunk:(c + 1) * chunk] = acc.astype(BF16)


def _proj_gate_body(h_ref, w_ref, b_ref, o_ref, *, chunk):
    h = h_ref[...]
    for c in range(w_ref.shape[1] // chunk):
        acc = jnp.dot(h, w_ref[:, c * chunk:(c + 1) * chunk], preferred_element_type=F32)
        z = acc + b_ref[:, c * chunk:(c + 1) * chunk]
        o_ref[:, c * chunk:(c + 1) * chunk] = (1.0 / (1.0 + jnp.exp(-z))).astype(BF16)


def _proj_call(body, h, w, extra, extra_specs, n_out, name, *, tm):
    t, d = h.shape
    return pl.pallas_call(
        body,
        grid=(t // tm,),
        in_specs=[pl.BlockSpec((tm, d), lambda i: (i, 0)), _resident(w.shape)] + extra_specs,
        out_specs=pl.BlockSpec((tm, n_out), lambda i: (i, 0)),
        out_shape=jax.ShapeDtypeStruct((t, n_out), BF16),
        compiler_params=pltpu.CompilerParams(
            dimension_semantics=("parallel",), vmem_limit_bytes=VMEM_LIMIT),
        name=name,
    )(h, w, *extra)


def _attn_a_body(sink_ref, q_ref, kp_ref, kc_ref, kn_ref, vp_ref, vc_ref, vn_ref, o_ref, *, tq, nq):
    i = pl.program_id(1)
    nk = tq + 2 * WINDOW
    row = lax.broadcasted_iota(jnp.int32, (tq, nk), 0)
    col = lax.broadcasted_iota(jnp.int32, (tq, nk), 1)
    rel = col - WINDOW - row
    valid = (rel >= -WINDOW) & (rel <= WINDOW)
    valid = valid & ((col >= WINDOW) | (i > 0))
    valid = valid & ((col < tq + WINDOW) | (i < nq - 1))
    k_all = jnp.concatenate([kp_ref[0], kc_ref[0], kn_ref[0]], axis=0)
    v_all = jnp.concatenate([vp_ref[0], vc_ref[0], vn_ref[0]], axis=0)
    for g in range(A_KV_HEADS):
        k = k_all[:, g * HEAD_DIM:(g + 1) * HEAD_DIM]
        v = v_all[:, g * HEAD_DIM:(g + 1) * HEAD_DIM]
        for r in range(A_REP):
            hd = g * A_REP + r
            q = q_ref[0, :, hd * HEAD_DIM:(hd + 1) * HEAD_DIM]
            s = lax.dot_general(q, k, (((1,), (1,)), ((), ())), preferred_element_type=F32)
            s = jnp.where(valid, s, NEG_INF)
            sk = sink_ref[hd] * LOG2E
            m = jnp.maximum(jnp.max(s, axis=-1, keepdims=True), sk)
            e = jnp.exp2(s - m)
            den = jnp.sum(e, axis=-1, keepdims=True) + jnp.exp2(sk - m)
            o = jnp.dot(e.astype(BF16), v, preferred_element_type=F32) / den
            o_ref[0, :, hd * HEAD_DIM:(hd + 1) * HEAD_DIM] = o.astype(BF16)


def _attn_a(rope3, v3, sink, *, tq=256):
    b, s, _ = rope3.shape
    nq = s // tq
    nblk = s // WINDOW
    per = tq // WINDOW
    ka_blk = A_Q // A_KV
    prev_map = lambda bi, i: (bi, jnp.maximum(i * per - 1, 0), ka_blk)
    cur_map = lambda bi, i: (bi, i, ka_blk)
    next_map = lambda bi, i: (bi, jnp.minimum((i + 1) * per, nblk - 1), ka_blk)
    vprev_map = lambda bi, i: (bi, jnp.maximum(i * per - 1, 0), 0)
    vcur_map = lambda bi, i: (bi, i, 0)
    vnext_map = lambda bi, i: (bi, jnp.minimum((i + 1) * per, nblk - 1), 0)
    return pl.pallas_call(
        functools.partial(_attn_a_body, tq=tq, nq=nq),
        grid=(b, nq),
        in_specs=[
            pl.BlockSpec(memory_space=pltpu.SMEM),
            pl.BlockSpec((1, tq, A_Q), lambda bi, i: (bi, i, 0)),
            pl.BlockSpec((1, WINDOW, A_KV), prev_map),
            pl.BlockSpec((1, tq, A_KV), cur_map),
            pl.BlockSpec((1, WINDOW, A_KV), next_map),
            pl.BlockSpec((1, WINDOW, A_KV), vprev_map),
            pl.BlockSpec((1, tq, A_KV), vcur_map),
            pl.BlockSpec((1, WINDOW, A_KV), vnext_map),
        ],
        out_specs=pl.BlockSpec((1, tq, A_Q), lambda bi, i: (bi, i, 0)),
        out_shape=jax.ShapeDtypeStruct((b, s, A_Q), BF16),
        compiler_params=pltpu.CompilerParams(
            dimension_semantics=("parallel", "parallel"), vmem_limit_bytes=VMEM_LIMIT),
        name="attn_a",
    )(sink, rope3, rope3, rope3, rope3, v3, v3, v3)


def _attn_b_body(q_ref, k_ref, v_ref, lam_ref, g_ref, o_ref,
                 s_scr, mx_scr, p_scr, a_scr, m_scr, l_scr, acc_scr, *, tk, nk, lam_init):
    m_scr[...] = jnp.full_like(m_scr, NEG_INF)
    l_scr[...] = jnp.zeros_like(l_scr)
    acc_scr[...] = jnp.zeros_like(acc_scr)
    reps = tk // LANES

    def scores(c, slot):
        off = pl.multiple_of(c * tk, tk)
        kc = k_ref[0, pl.ds(off, tk), :]
        for comp in range(2):
            q = q_ref[0, :, comp * HEAD_DIM:(comp + 1) * HEAD_DIM]
            k = kc[:, comp * HEAD_DIM:(comp + 1) * HEAD_DIM]
            s = lax.dot_general(q, k, (((1,), (1,)), ((), ())), preferred_element_type=F32)
            s_scr[slot, comp] = s
            mx = s[:, :LANES]
            for j in range(1, reps):
                mx = jnp.maximum(mx, s[:, j * LANES:(j + 1) * LANES])
            mx_scr[slot, comp] = mx

    def softmax(slot):
        for comp in range(2):
            m_prev = m_scr[comp]
            m_new = jnp.maximum(m_prev, jnp.max(mx_scr[slot, comp], axis=-1, keepdims=True))
            a_scr[slot, comp] = jnp.exp2(m_prev - m_new)
            m_scr[comp] = m_new
            e = jnp.exp2(s_scr[slot, comp] - pltpu.repeat(m_new, reps, axis=1))
            e_sum = e[:, :LANES]
            for j in range(1, reps):
                e_sum = e_sum + e[:, j * LANES:(j + 1) * LANES]
            l_scr[comp] = a_scr[slot, comp] * l_scr[comp] + e_sum
            p_scr[slot, comp] = e.astype(BF16)

    def pv(c, slot):
        off = pl.multiple_of(c * tk, tk)
        vc = v_ref[0, pl.ds(off, tk), :]
        for comp in range(2):
            acc_scr[comp] = pltpu.repeat(a_scr[slot, comp], 2, axis=1) * acc_scr[comp] + jnp.dot(
                p_scr[slot, comp], vc, preferred_element_type=F32)

    def steady_pair(j, carry):
        i = 2 * j + 1
        pv(i - 1, 0)
        scores(i + 1, 0)
        softmax(1)
        pv(i, 1)
        scores(i + 2, 1)
        softmax(0)
        return carry

    scores(0, 0)
    scores(1, 1)
    softmax(0)
    lax.fori_loop(0, (nk - 2) // 2, steady_pair, 0)
    softmax(1)
    pv(nk - 2, 0)
    pv(nk - 1, 1)

    lam_v = lam_ref[...]
    lam = (jnp.exp(jnp.sum(lam_v[0:1] * lam_v[1:2], axis=-1, keepdims=True))
           - jnp.exp(jnp.sum(lam_v[2:3] * lam_v[3:4], axis=-1, keepdims=True))
           + lam_init)
    l0 = jnp.sum(l_scr[0], axis=-1, keepdims=True)
    l1 = jnp.sum(l_scr[1], axis=-1, keepdims=True)
    o = acc_scr[0] / l0 - lam * (acc_scr[1] / l1)
    o_ref[0] = (_rms(o, g_ref[...]) * (1.0 - lam_init)).astype(BF16)


def _attn_b(rope3, v3, lam_vecs, subln_g, lam_init, *, tq=512, tk=512):
    b, s, _ = rope3.shape
    nk = s // tk
    assert nk % 2 == 0 and nk >= 4
    w = 2 * HEAD_DIM
    qb_blk = (A_Q + A_KV) // w
    kb_blk = qb_blk + B_HEADS
    vb_blk = A_KV // w
    return pl.pallas_call(
        functools.partial(_attn_b_body, tk=tk, nk=nk, lam_init=lam_init),
        grid=(b, B_HEADS, s // tq),
        in_specs=[
            pl.BlockSpec((1, tq, w), lambda bi, h, i: (bi, i, qb_blk + h)),
            pl.BlockSpec((1, s, w), lambda bi, h, i: (bi, 0, kb_blk + h)),
            pl.BlockSpec((1, s, w), lambda bi, h, i: (bi, 0, vb_blk + h)),
            pl.BlockSpec((4, HEAD_DIM), lambda bi, h, i: (0, 0)),
            pl.BlockSpec((1, w), lambda bi, h, i: (0, 0)),
        ],
        out_specs=pl.BlockSpec((1, tq, w), lambda bi, h, i: (bi, i, h)),
        out_shape=jax.ShapeDtypeStruct((b, s, B_V), BF16),
        scratch_shapes=[pltpu.VMEM((2, 2, tq, tk), F32),
                        pltpu.VMEM((2, 2, tq, LANES), F32),
                        pltpu.VMEM((2, 2, tq, tk), BF16),
                        pltpu.VMEM((2, 2, tq, LANES), F32),
                        pltpu.VMEM((2, tq, LANES), F32),
                        pltpu.VMEM((2, tq, LANES), F32),
                        pltpu.VMEM((2, tq, w), F32)],
        compiler_params=pltpu.CompilerParams(
            dimension_semantics=("parallel", "parallel", "arbitrary"),
            vmem_limit_bytes=VMEM_LIMIT),
        name="attn_b",
    )(rope3, rope3, v3, lam_vecs, subln_g)


def _merge_body(oa_ref, ob_ref, gate_ref, x_ref, wa_ref, wb_ref, wo_ref, g_ref, o_ref):
    pa = jnp.dot(oa_ref[...], wa_ref[...], preferred_element_type=F32)
    pb = jnp.dot(ob_ref[...], wb_ref[...], preferred_element_type=F32)
    ga = gate_ref[:, :D_MODEL].astype(F32)
    gb = gate_ref[:, D_MODEL:].astype(F32)
    merged = (ga * pa + gb * pb).astype(BF16)
    y = jnp.dot(merged, wo_ref[...], preferred_element_type=F32)
    o_ref[...] = x_ref[...] + _rms(y, g_ref[...])


def _merge(oa, ob, gates, x1, wa, wb, wo, g_post, *, tm=512):
    t, d = x1.shape
    rows = lambda n: pl.BlockSpec((tm, n), lambda i: (i, 0))
    return pl.pallas_call(
        _merge_body,
        grid=(t // tm,),
        in_specs=[rows(A_Q), rows(B_V), rows(GATE_COLS), rows(d),
                  _resident(wa.shape), _resident(wb.shape), _resident(wo.shape),
                  pl.BlockSpec((1, d), lambda i: (0, 0))],
        out_specs=rows(d),
        out_shape=jax.ShapeDtypeStruct((t, d), F32),
        compiler_params=pltpu.CompilerParams(
            dimension_semantics=("parallel",), vmem_limit_bytes=VMEM_LIMIT),
        name="merge",
    )(oa, ob, gates, x1, wa, wb, wo, g_post)


def _rope_tables(seq):
    pos = jnp.arange(seq, dtype=F32)
    inv = ROPE_THETA ** (-jnp.arange(0, HEAD_DIM, 2, dtype=F32) / HEAD_DIM)
    ang = pos[:, None] * inv[None, :]
    cos, sin = jnp.cos(ang), jnp.sin(ang)
    return jnp.concatenate([cos, cos], axis=-1), jnp.concatenate([-sin, sin], axis=-1)


def kernel(x, ffn1_pre_g, ffn1_w_gate, ffn1_w_up, ffn1_w_down, ffn1_post_g, mix_pre_g, w_in, gate_bias, sink_logit, lambda_q1, lambda_k1, lambda_q2, lambda_k2, subln_g, w_proj_a, w_proj_b, w_out, mix_post_g, ffn2_pre_g, ffn2_w_gate, ffn2_w_up, ffn2_w_down, ffn2_post_g):
    b, s, d = x.shape
    t = b * s
    depth = ffn1_pre_g.shape[0]
    cos2, sin2 = _rope_tables(s)
    xt = x.reshape(t, d)
    q_scaled = (QK_SCALE,) * A_Q_HEADS + (1.0,) * A_KV_HEADS + (QK_SCALE,) * (2 * B_HEADS) + (1.0,) * (2 * B_HEADS)
    tm_proj = 512
    for l in range(depth):
        lam_init = 0.8 - 0.6 * math.exp(-0.3 * l)
        row = lambda p: p[l:l + 1]
        bf = lambda p: p[l].astype(BF16)

        xt, h = _ffn(xt, row(ffn1_pre_g), bf(ffn1_w_gate), bf(ffn1_w_up), bf(ffn1_w_down),
                     row(ffn1_post_g), row(mix_pre_g))

        w = w_in[l]
        qa_end, ka_end, va_end = A_Q, A_Q + A_KV, A_Q + 2 * A_KV
        qb_end = va_end + B_QK
        kb_end = qb_end + B_QK
        vb_end = kb_end + B_V
        w_rope = jnp.concatenate([w[:, :ka_end], w[:, va_end:kb_end]], axis=1).astype(BF16)
        w_v = jnp.concatenate([w[:, ka_end:va_end], w[:, kb_end:vb_end]], axis=1).astype(BF16)
        w_gate = w[:, vb_end:].astype(BF16)

        ns = s // tm_proj
        tab = pl.BlockSpec((tm_proj, HEAD_DIM), lambda i: (i % ns, 0))
        rope = _proj_call(functools.partial(_proj_rope_body, head_scales=q_scaled),
                          h, w_rope, [cos2, sin2], [tab, tab], ROPE_COLS, "proj_rope", tm=tm_proj)
        vals = _proj_call(functools.partial(_proj_plain_body, chunk=MXU_N),
                          h, w_v, [], [], V_COLS, "proj_v", tm=tm_proj)
        gates = _proj_call(functools.partial(_proj_gate_body, chunk=2 * MXU_N),
                           h, w_gate, [gate_bias[l:l + 1]], [_resident((1, GATE_COLS))],
                           GATE_COLS, "proj_gate", tm=tm_proj)

        rope3 = rope.reshape(b, s, ROPE_COLS)
        v3 = vals.reshape(b, s, V_COLS)
        out_a = _attn_a(rope3, v3, sink_logit[l])
        lam_vecs = jnp.stack([lambda_q1[l], lambda_k1[l], lambda_q2[l], lambda_k2[l]]).astype(F32)
        out_b = _attn_b(rope3, v3, lam_vecs, row(subln_g), lam_init)

        xt = _merge(out_a.reshape(t, A_Q), out_b.reshape(t, B_V), gates, xt,
                    bf(w_proj_a), bf(w_proj_b), bf(w_out), row(mix_post_g))

        xt = _ffn(xt, row(ffn2_pre_g), bf(ffn2_w_gate), bf(ffn2_w_up), bf(ffn2_w_down),
                  row(ffn2_post_g))
    return xt.reshape(b, s, d)
```

```python
import functools
import math

import jax
import jax.numpy as jnp
from jax import lax
from jax.experimental import pallas as pl
from jax.experimental.pallas import tpu as pltpu

D_MODEL = 2048
HEAD_DIM = 128
A_Q_HEADS = 8
A_KV_HEADS = 2
A_REP = A_Q_HEADS // A_KV_HEADS
WINDOW = 128
B_HEADS = 4
D_FF = 5632
ROPE_THETA = 10000.0
EPS = 1e-6
NEG_INF = -1e30
LOG2E = 1.4426950408889634
QK_SCALE = HEAD_DIM ** -0.5 * LOG2E

A_Q = A_Q_HEADS * HEAD_DIM
A_KV = A_KV_HEADS * HEAD_DIM
B_QK = B_HEADS * 2 * HEAD_DIM
B_V = B_HEADS * 2 * HEAD_DIM
ROPE_COLS = A_Q + A_KV + 2 * B_QK
V_COLS = A_KV + B_V
GATE_COLS = 2 * D_MODEL

LANES = 128
MXU_N = 256
VMEM_LIMIT = 56 * 1024 * 1024

F32 = jnp.float32
BF16 = jnp.bfloat16


def _rms(x, g):
    ms = jnp.mean(x * x, axis=-1, keepdims=True)
    return x * lax.rsqrt(ms + EPS) * g


def _lane_tile(x, n):
    return jnp.concatenate([x] * n, axis=1)


def _resident(shape):
    return pl.BlockSpec(shape, lambda *_: (0,) * len(shape), pipeline_mode=pl.Buffered(1))


def _ffn_body(*refs, nf, emit_next):
    if emit_next:
        (x_ref, gpre_ref, wg_ref, wu_ref, wd_ref, gpost_ref, gnext_ref,
         o_ref, hn_ref, h_scr) = refs
    else:
        (x_ref, gpre_ref, wg_ref, wu_ref, wd_ref, gpost_ref, o_ref, h_scr) = refs
    f = pl.program_id(1)

    @pl.when(f == 0)
    def _():
        h_scr[...] = _rms(x_ref[...], gpre_ref[...]).astype(BF16)
        o_ref[...] = jnp.zeros_like(o_ref)

    h = h_scr[...]
    gate = jnp.dot(h, wg_ref[...], preferred_element_type=F32)
    up = jnp.dot(h, wu_ref[...], preferred_element_type=F32)
    a = gate * (1.0 / (1.0 + jnp.exp(-gate))) * up
    o_ref[...] += jnp.dot(a.astype(BF16), wd_ref[...], preferred_element_type=F32)

    @pl.when(f == nf - 1)
    def _():
        xo = x_ref[...] + 0.5 * _rms(o_ref[...], gpost_ref[...])
        o_ref[...] = xo
        if emit_next:
            hn_ref[...] = _rms(xo, gnext_ref[...]).astype(BF16)


def _ffn(x, g_pre, wg, wu, wd, g_post, g_next=None, *, tm=512, tf=512):
    t, d = x.shape
    nf = D_FF // tf
    emit_next = g_next is not None
    row = pl.BlockSpec((tm, d), lambda i, f: (i, 0))
    vec = pl.BlockSpec((1, d), lambda i, f: (0, 0))
    in_specs = [row, vec,
                pl.BlockSpec((d, tf), lambda i, f: (0, f)),
                pl.BlockSpec((d, tf), lambda i, f: (0, f)),
                pl.BlockSpec((tf, d), lambda i, f: (f, 0)),
                vec]
    args = [x, g_pre, wg, wu, wd, g_post]
    out_shape = [jax.ShapeDtypeStruct((t, d), F32)]
    out_specs = [row]
    if emit_next:
        in_specs.append(vec)
        args.append(g_next)
        out_shape.append(jax.ShapeDtypeStruct((t, d), BF16))
        out_specs.append(row)
    res = pl.pallas_call(
        functools.partial(_ffn_body, nf=nf, emit_next=emit_next),
        grid=(t // tm, nf),
        in_specs=in_specs,
        out_specs=out_specs,
        out_shape=out_shape,
        scratch_shapes=[pltpu.VMEM((tm, d), BF16)],
        compiler_params=pltpu.CompilerParams(
            dimension_semantics=("parallel", "arbitrary"), vmem_limit_bytes=VMEM_LIMIT),
        name="ffn_next" if emit_next else "ffn",
    )(*args)
    return res if emit_next else res[0]


def _proj_rope_body(h_ref, w_ref, cos_ref, sin_ref, o_ref, *, head_scales):
    h = h_ref[...]
    cos = cos_ref[...]
    sin = sin_ref[...]
    heads_per_chunk = MXU_N // HEAD_DIM
    for c in range(len(head_scales) // heads_per_chunk):
        acc = jnp.dot(h, w_ref[:, c * MXU_N:(c + 1) * MXU_N], preferred_element_type=F32)
        for k in range(heads_per_chunk):
            hd = c * heads_per_chunk + k
            a = acc[:, k * HEAD_DIM:(k + 1) * HEAD_DIM]
            r = a * cos + pltpu.roll(a, HEAD_DIM // 2, 1) * sin
            if head_scales[hd] != 1.0:
                r = r * head_scales[hd]
            o_ref[:, hd * HEAD_DIM:(hd + 1) * HEAD_DIM] = r.astype(BF16)


def _proj_plain_body(h_ref, w_ref, o_ref, *, chunk):
    h = h_ref[...]
    for c in range(w_ref.shape[1] // chunk):
        acc = jnp.dot(h, w_ref[:, c * chunk:(c + 1) * chunk], preferred_element_type=F32)
        o_ref[:, c * chunk:(c + 1) * chunk] = acc.astype(BF16)


def _proj_gate_body(h_ref, w_ref, b_ref, o_ref, *, chunk):
    h = h_ref[...]
    for c in range(w_ref.shape[1] // chunk):
        acc = jnp.dot(h, w_ref[:, c * chunk:(c + 1) * chunk], preferred_element_type=F32)
        z = acc + b_ref[:, c * chunk:(c + 1) * chunk]
        o_ref[:, c * chunk:(c + 1) * chunk] = (1.0 / (1.0 + jnp.exp(-z))).astype(BF16)


def _proj_call(body, h, w, extra, extra_specs, n_out, name, *, tm):
    t, d = h.shape
    return pl.pallas_call(
        body,
        grid=(t // tm,),
        in_specs=[pl.BlockSpec((tm, d), lambda i: (i, 0)), _resident(w.shape)] + extra_specs,
        out_specs=pl.BlockSpec((tm, n_out), lambda i: (i, 0)),
        out_shape=jax.ShapeDtypeStruct((t, n_out), BF16),
        compiler_params=pltpu.CompilerParams(
            dimension_semantics=("parallel",), vmem_limit_bytes=VMEM_LIMIT),
        name=name,
    )(h, w, *extra)


def _attn_a_body(sink_ref, q_ref, kp_ref, kc_ref, kn_ref, vp_ref, vc_ref, vn_ref, o_ref, *, tq, nq):
    i = pl.program_id(1)
    nk = tq + 2 * WINDOW
    row = lax.broadcasted_iota(jnp.int32, (tq, nk), 0)
    col = lax.broadcasted_iota(jnp.int32, (tq, nk), 1)
    rel = col - WINDOW - row
    valid = (rel >= -WINDOW) & (rel <= WINDOW)
    valid = valid & ((col >= WINDOW) | (i > 0))
    valid = valid & ((col < tq + WINDOW) | (i < nq - 1))
    k_all = jnp.concatenate([kp_ref[0], kc_ref[0], kn_ref[0]], axis=0)
    v_all = jnp.concatenate([vp_ref[0], vc_ref[0], vn_ref[0]], axis=0)
    for g in range(A_KV_HEADS):
        k = k_all[:, g * HEAD_DIM:(g + 1) * HEAD_DIM]
        v = v_all[:, g * HEAD_DIM:(g + 1) * HEAD_DIM]
        for r in range(A_REP):
            hd = g * A_REP + r
            q = q_ref[0, :, hd * HEAD_DIM:(hd + 1) * HEAD_DIM]
            s = lax.dot_general(q, k, (((1,), (1,)), ((), ())), preferred_element_type=F32)
            s = jnp.where(valid, s, NEG_INF)
            sk = sink_ref[hd] * LOG2E
            m = jnp.maximum(jnp.max(s, axis=-1, keepdims=True), sk)
            e = jnp.exp2(s - m)
            den = jnp.sum(e, axis=-1, keepdims=True) + jnp.exp2(sk - m)
            o = jnp.dot(e.astype(BF16), v, preferred_element_type=F32) / den
            o_ref[0, :, hd * HEAD_DIM:(hd + 1) * HEAD_DIM] = o.astype(BF16)


def _attn_a(rope3, v3, sink, *, tq=256):
    b, s, _ = rope3.shape
    nq = s // tq
    nblk = s // WINDOW
    per = tq // WINDOW
    ka_blk = A_Q // A_KV
    prev_map = lambda bi, i: (bi, jnp.maximum(i * per - 1, 0), ka_blk)
    cur_map = lambda bi, i: (bi, i, ka_blk)
    next_map = lambda bi, i: (bi, jnp.minimum((i + 1) * per, nblk - 1), ka_blk)
    vprev_map = lambda bi, i: (bi, jnp.maximum(i * per - 1, 0), 0)
    vcur_map = lambda bi, i: (bi, i, 0)
    vnext_map = lambda bi, i: (bi, jnp.minimum((i + 1) * per, nblk - 1), 0)
    return pl.pallas_call(
        functools.partial(_attn_a_body, tq=tq, nq=nq),
        grid=(b, nq),
        in_specs=[
            pl.BlockSpec(memory_space=pltpu.SMEM),
            pl.BlockSpec((1, tq, A_Q), lambda bi, i: (bi, i, 0)),
            pl.BlockSpec((1, WINDOW, A_KV), prev_map),
            pl.BlockSpec((1, tq, A_KV), cur_map),
            pl.BlockSpec((1, WINDOW, A_KV), next_map),
            pl.BlockSpec((1, WINDOW, A_KV), vprev_map),
            pl.BlockSpec((1, tq, A_KV), vcur_map),
            pl.BlockSpec((1, WINDOW, A_KV), vnext_map),
        ],
        out_specs=pl.BlockSpec((1, tq, A_Q), lambda bi, i: (bi, i, 0)),
        out_shape=jax.ShapeDtypeStruct((b, s, A_Q), BF16),
        compiler_params=pltpu.CompilerParams(
            dimension_semantics=("parallel", "parallel"), vmem_limit_bytes=VMEM_LIMIT),
        name="attn_a",
    )(sink, rope3, rope3, rope3, rope3, v3, v3, v3)


def _attn_b_body(q_ref, k_ref, v_ref, lam_ref, g_ref, o_ref,
                 s_scr, mx_scr, p_scr, a_scr, m_scr, l_scr, acc_scr, *, tk, nk, lam_init):
    m_scr[...] = jnp.full_like(m_scr, NEG_INF)
    l_scr[...] = jnp.zeros_like(l_scr)
    acc_scr[...] = jnp.zeros_like(acc_scr)
    reps = tk // LANES

    def scores(c, slot):
        off = pl.multiple_of(c * tk, tk)
        kc = k_ref[0, pl.ds(off, tk), :]
        for comp in range(2):
            q = q_ref[0, :, comp * HEAD_DIM:(comp + 1) * HEAD_DIM]
            k = kc[:, comp * HEAD_DIM:(comp + 1) * HEAD_DIM]
            s = lax.dot_general(q, k, (((1,), (1,)), ((), ())), preferred_element_type=F32)
            s_scr[slot, comp] = s
            mx = s[:, :LANES]
            for j in range(1, reps):
                mx = jnp.maximum(mx, s[:, j * LANES:(j + 1) * LANES])
            mx_scr[slot, comp] = mx

    def softmax(slot):
        for comp in range(2):
            m_prev = m_scr[comp]
            m_new = jnp.maximum(m_prev, jnp.max(mx_scr[slot, comp], axis=-1, keepdims=True))
            a_scr[slot, comp] = jnp.exp2(m_prev - m_new)
            m_scr[comp] = m_new
            e = jnp.exp2(s_scr[slot, comp] - _lane_tile(m_new, reps))
            e_sum = e[:, :LANES]
            for j in range(1, reps):
                e_sum = e_sum + e[:, j * LANES:(j + 1) * LANES]
            l_scr[comp] = a_scr[slot, comp] * l_scr[comp] + e_sum
            p_scr[slot, comp] = e.astype(BF16)

    def pv(c, slot):
        off = pl.multiple_of(c * tk, tk)
        vc = v_ref[0, pl.ds(off, tk), :]
        for comp in range(2):
            acc_scr[comp] = _lane_tile(a_scr[slot, comp], 2) * acc_scr[comp] + jnp.dot(
                p_scr[slot, comp], vc, preferred_element_type=F32)

    def steady_pair(i):
        pv(i - 1, 0)
        scores(i + 1, 0)
        softmax(1)
        pv(i, 1)
        scores(i + 2, 1)
        softmax(0)

    def steady_quad(j, carry):
        steady_pair(4 * j + 1)
        steady_pair(4 * j + 3)
        return carry

    scores(0, 0)
    scores(1, 1)
    softmax(0)
    n_pairs = (nk - 2) // 2
    lax.fori_loop(0, n_pairs // 2, steady_quad, 0)
    if n_pairs % 2:
        steady_pair(nk - 3)
    softmax(1)
    pv(nk - 2, 0)
    pv(nk - 1, 1)

    lam_v = lam_ref[...]
    lam = (jnp.exp(jnp.sum(lam_v[0:1] * lam_v[1:2], axis=-1, keepdims=True))
           - jnp.exp(jnp.sum(lam_v[2:3] * lam_v[3:4], axis=-1, keepdims=True))
           + lam_init)
    l0 = jnp.sum(l_scr[0], axis=-1, keepdims=True)
    l1 = jnp.sum(l_scr[1], axis=-1, keepdims=True)
    o = acc_scr[0] / l0 - lam * (acc_scr[1] / l1)
    o_ref[0] = (_rms(o, g_ref[...]) * (1.0 - lam_init)).astype(BF16)


def _attn_b(rope3, v3, lam_vecs, subln_g, lam_init, *, tq=512, tk=512):
    b, s, _ = rope3.shape
    nk = s // tk
    assert nk % 2 == 0 and nk >= 4
    w = 2 * HEAD_DIM
    qb_blk = (A_Q + A_KV) // w
    kb_blk = qb_blk + B_HEADS
    vb_blk = A_KV // w
    return pl.pallas_call(
        functools.partial(_attn_b_body, tk=tk, nk=nk, lam_init=lam_init),
        grid=(b, B_HEADS, s // tq),
        in_specs=[
            pl.BlockSpec((1, tq, w), lambda bi, h, i: (bi, i, qb_blk + h)),
            pl.BlockSpec((1, s, w), lambda bi, h, i: (bi, 0, kb_blk + h)),
            pl.BlockSpec((1, s, w), lambda bi, h, i: (bi, 0, vb_blk + h)),
            pl.BlockSpec((4, HEAD_DIM), lambda bi, h, i: (0, 0)),
            pl.BlockSpec((1, w), lambda bi, h, i: (0, 0)),
        ],
        out_specs=pl.BlockSpec((1, tq, w), lambda bi, h, i: (bi, i, h)),
        out_shape=jax.ShapeDtypeStruct((b, s, B_V), BF16),
        scratch_shapes=[pltpu.VMEM((2, 2, tq, tk), F32),
                        pltpu.VMEM((2, 2, tq, LANES), F32),
                        pltpu.VMEM((2, 2, tq, tk), BF16),
                        pltpu.VMEM((2, 2, tq, LANES), F32),
                        pltpu.VMEM((2, tq, LANES), F32),
                        pltpu.VMEM((2, tq, LANES), F32),
                        pltpu.VMEM((2, tq, w), F32)],
        compiler_params=pltpu.CompilerParams(
            dimension_semantics=("parallel", "parallel", "arbitrary"),
            vmem_limit_bytes=VMEM_LIMIT),
        name="attn_b",
    )(rope3, rope3, v3, lam_vecs, subln_g)


def _merge_body(oa_ref, ob_ref, gate_ref, x_ref, wa_ref, wb_ref, wo_ref, g_ref, o_ref):
    pa = jnp.dot(oa_ref[...], wa_ref[...], preferred_element_type=F32)
    pb = jnp.dot(ob_ref[...], wb_ref[...], preferred_element_type=F32)
    ga = gate_ref[:, :D_MODEL].astype(F32)
    gb = gate_ref[:, D_MODEL:].astype(F32)
    merged = (ga * pa + gb * pb).astype(BF16)
    y = jnp.dot(merged, wo_ref[...], preferred_element_type=F32)
    o_ref[...] = x_ref[...] + _rms(y, g_ref[...])


def _merge(oa, ob, gates, x1, wa, wb, wo, g_post, *, tm=512):
    t, d = x1.shape
    rows = lambda n: pl.BlockSpec((tm, n), lambda i: (i, 0))
    return pl.pallas_call(
        _merge_body,
        grid=(t // tm,),
        in_specs=[rows(A_Q), rows(B_V), rows(GATE_COLS), rows(d),
                  _resident(wa.shape), _resident(wb.shape), _resident(wo.shape),
                  pl.BlockSpec((1, d), lambda i: (0, 0))],
        out_specs=rows(d),
        out_shape=jax.ShapeDtypeStruct((t, d), F32),
        compiler_params=pltpu.CompilerParams(
            dimension_semantics=("parallel",), vmem_limit_bytes=VMEM_LIMIT),
        name="merge",
    )(oa, ob, gates, x1, wa, wb, wo, g_post)


def _rope_tables(seq):
    pos = jnp.arange(seq, dtype=F32)
    inv = ROPE_THETA ** (-jnp.arange(0, HEAD_DIM, 2, dtype=F32) / HEAD_DIM)
    ang = pos[:, None] * inv[None, :]
    cos, sin = jnp.cos(ang), jnp.sin(ang)
    return jnp.concatenate([cos, cos], axis=-1), jnp.concatenate([-sin, sin], axis=-1)


def kernel(x, ffn1_pre_g, ffn1_w_gate, ffn1_w_up, ffn1_w_down, ffn1_post_g, mix_pre_g, w_in, gate_bias, sink_logit, lambda_q1, lambda_k1, lambda_q2, lambda_k2, subln_g, w_proj_a, w_proj_b, w_out, mix_post_g, ffn2_pre_g, ffn2_w_gate, ffn2_w_up, ffn2_w_down, ffn2_post_g):
    b, s, d = x.shape
    t = b * s
    depth = ffn1_pre_g.shape[0]
    cos2, sin2 = _rope_tables(s)
    xt = x.reshape(t, d)
    q_scaled = (QK_SCALE,) * A_Q_HEADS + (1.0,) * A_KV_HEADS + (QK_SCALE,) * (2 * B_HEADS) + (1.0,) * (2 * B_HEADS)
    tm_proj = 512
    for l in range(depth):
        lam_init = 0.8 - 0.6 * math.exp(-0.3 * l)
        row = lambda p: p[l:l + 1]
        bf = lambda p: p[l].astype(BF16)

        xt, h = _ffn(xt, row(ffn1_pre_g), bf(ffn1_w_gate), bf(ffn1_w_up), bf(ffn1_w_down),
                     row(ffn1_post_g), row(mix_pre_g))

        w = w_in[l]
        qa_end, ka_end, va_end = A_Q, A_Q + A_KV, A_Q + 2 * A_KV
        qb_end = va_end + B_QK
        kb_end = qb_end + B_QK
        vb_end = kb_end + B_V
        w_rope = jnp.concatenate([w[:, :ka_end], w[:, va_end:kb_end]], axis=1).astype(BF16)
        w_v = jnp.concatenate([w[:, ka_end:va_end], w[:, kb_end:vb_end]], axis=1).astype(BF16)
        w_gate = w[:, vb_end:].astype(BF16)

        ns = s // tm_proj
        tab = pl.BlockSpec((tm_proj, HEAD_DIM), lambda i: (i % ns, 0))
        rope = _proj_call(functools.partial(_proj_rope_body, head_scales=q_scaled),
                          h, w_rope, [cos2, sin2], [tab, tab], ROPE_COLS, "proj_rope", tm=tm_proj)
        vals = _proj_call(functools.partial(_proj_plain_body, chunk=MXU_N),
                          h, w_v, [], [], V_COLS, "proj_v", tm=tm_proj)
        gates = _proj_call(functools.partial(_proj_gate_body, chunk=2 * MXU_N),
                           h, w_gate, [gate_bias[l:l + 1]], [_resident((1, GATE_COLS))],
                           GATE_COLS, "proj_gate", tm=tm_proj)

        rope3 = rope.reshape(b, s, ROPE_COLS)
        v3 = vals.reshape(b, s, V_COLS)
        out_a = _attn_a(rope3, v3, sink_logit[l])
        lam_vecs = jnp.stack([lambda_q1[l], lambda_k1[l], lambda_q2[l], lambda_k2[l]]).astype(F32)
        out_b = _attn_b(rope3, v3, lam_vecs, row(subln_g), lam_init)

        xt = _merge(out_a.reshape(t, A_Q), out_b.reshape(t, B_V), gates, xt,
                    bf(w_proj_a), bf(w_proj_b), bf(w_out), row(mix_post_g))

        xt = _ffn(xt, row(ffn2_pre_g), bf(ffn2_w_gate), bf(ffn2_w_up), bf(ffn2_w_down),
                  row(ffn2_post_g))
    return xt.reshape(b, s, d)
```

```python
import functools
import math

import jax
import jax.numpy as jnp
from jax import lax
from jax.experimental import pallas as pl
from jax.experimental.pallas import tpu as pltpu

D_MODEL = 2048
HEAD_DIM = 128
A_Q_HEADS = 8
A_KV_HEADS = 2
A_REP = A_Q_HEADS // A_KV_HEADS
WINDOW = 128
B_HEADS = 4
D_FF = 5632
ROPE_THETA = 10000.0
EPS = 1e-6
NEG_INF = -1e30
LOG2E = 1.4426950408889634
QK_SCALE = HEAD_DIM ** -0.5 * LOG2E

A_Q = A_Q_HEADS * HEAD_DIM
A_KV = A_KV_HEADS * HEAD_DIM
B_QK = B_HEADS * 2 * HEAD_DIM
B_V = B_HEADS * 2 * HEAD_DIM
ROPE_COLS = A_Q + A_KV + 2 * B_QK
V_COLS = A_KV + B_V
GATE_COLS = 2 * D_MODEL

LANES = 128
MXU_N = 256
VMEM_LIMIT = 56 * 1024 * 1024

F32 = jnp.float32
BF16 = jnp.bfloat16


def _rms(x, g):
    ms = jnp.mean(x * x, axis=-1, keepdims=True)
    return x * lax.rsqrt(ms + EPS) * g


def _lane_tile(x, n):
    return jnp.concatenate([x] * n, axis=1)


def _resident(shape):
    return pl.BlockSpec(shape, lambda *_: (0,) * len(shape), pipeline_mode=pl.Buffered(1))


def _ffn_body(*refs, nf, tf, n_sub_last, emit_next):
    if emit_next:
        (x_ref, gpre_ref, wg_ref, wu_ref, wd_ref, gpost_ref, gnext_ref,
         o_ref, hn_ref, h_scr) = refs
    else:
        (x_ref, gpre_ref, wg_ref, wu_ref, wd_ref, gpost_ref, o_ref, h_scr) = refs
    f = pl.program_id(1)
    n_sub = wg_ref.shape[1] // tf

    def sub_blocks(n, h, first=False):
        for j in range(n):
            cols = slice(j * tf, (j + 1) * tf)
            gate = jnp.dot(h, wg_ref[:, cols], preferred_element_type=F32)
            up = jnp.dot(h, wu_ref[:, cols], preferred_element_type=F32)
            a = gate * (1.0 / (1.0 + jnp.exp(-gate))) * up
            y = jnp.dot(a.astype(BF16), wd_ref[cols, :], preferred_element_type=F32)
            if first and j == 0:
                o_ref[...] = y
            else:
                o_ref[...] += y

    @pl.when(f == 0)
    def _():
        h = _rms(x_ref[...], gpre_ref[...]).astype(BF16)
        h_scr[...] = h
        sub_blocks(n_sub, h, first=True)

    @pl.when((f > 0) & (f < nf - 1))
    def _():
        sub_blocks(n_sub, h_scr[...])

    @pl.when(f == nf - 1)
    def _():
        sub_blocks(n_sub_last, h_scr[...])
        xo = x_ref[...] + 0.5 * _rms(o_ref[...], gpost_ref[...])
        o_ref[...] = xo
        if emit_next:
            hn_ref[...] = _rms(xo, gnext_ref[...]).astype(BF16)


def _ffn(x, g_pre, wg, wu, wd, g_post, g_next=None, *, tm=512, tf=512, n_sub=2):
    t, d = x.shape
    blk = tf * n_sub
    nf = pl.cdiv(D_FF, blk)
    n_sub_last = (D_FF - (nf - 1) * blk) // tf
    assert (nf - 1) * blk + n_sub_last * tf == D_FF
    emit_next = g_next is not None
    row = pl.BlockSpec((tm, d), lambda i, f: (i, 0))
    vec = pl.BlockSpec((1, d), lambda i, f: (0, 0))
    in_specs = [row, vec,
                pl.BlockSpec((d, blk), lambda i, f: (0, f)),
                pl.BlockSpec((d, blk), lambda i, f: (0, f)),
                pl.BlockSpec((blk, d), lambda i, f: (f, 0)),
                vec]
    args = [x, g_pre, wg, wu, wd, g_post]
    out_shape = [jax.ShapeDtypeStruct((t, d), F32)]
    out_specs = [row]
    if emit_next:
        in_specs.append(vec)
        args.append(g_next)
        out_shape.append(jax.ShapeDtypeStruct((t, d), BF16))
        out_specs.append(row)
    res = pl.pallas_call(
        functools.partial(_ffn_body, nf=nf, tf=tf, n_sub_last=n_sub_last, emit_next=emit_next),
        grid=(t // tm, nf),
        in_specs=in_specs,
        out_specs=out_specs,
        out_shape=out_shape,
        scratch_shapes=[pltpu.VMEM((tm, d), BF16)],
        compiler_params=pltpu.CompilerParams(
            dimension_semantics=("parallel", "arbitrary"), vmem_limit_bytes=VMEM_LIMIT),
        name="ffn_next" if emit_next else "ffn",
    )(*args)
    return res if emit_next else res[0]


def _proj_rope_body(h_ref, w_ref, cos_ref, sin_ref, o_ref, *, head_scales):
    h = h_ref[...]
    cos = cos_ref[...]
    sin = sin_ref[...]
    heads_per_chunk = MXU_N // HEAD_DIM
    for c in range(len(head_scales) // heads_per_chunk):
        acc = jnp.dot(h, w_ref[:, c * MXU_N:(c + 1) * MXU_N], preferred_element_type=F32)
        for k in range(heads_per_chunk):
            hd = c * heads_per_chunk + k
            a = acc[:, k * HEAD_DIM:(k + 1) * HEAD_DIM]
            r = a * cos + pltpu.roll(a, HEAD_DIM // 2, 1) * sin
            if head_scales[hd] != 1.0:
                r = r * head_scales[hd]
            o_ref[:, hd * HEAD_DIM:(hd + 1) * HEAD_DIM] = r.astype(BF16)


def _proj_plain_body(h_ref, w_ref, o_ref, *, chunk):
    h = h_ref[...]
    for c in range(w_ref.shape[1] // chunk):
        acc = jnp.dot(h, w_ref[:, c * chunk:(c + 1) * chunk], preferred_element_type=F32)
        o_ref[:, c * chunk:(c + 1) * chunk] = acc.astype(BF16)


def _proj_gate_body(h_ref, w_ref, b_ref, o_ref, *, chunk):
    h = h_ref[...]
    for c in range(w_ref.shape[1] // chunk):
        acc = jnp.dot(h, w_ref[:, c * chunk:(c + 1) * chunk], preferred_element_type=F32)
        z = acc + b_ref[:, c * chunk:(c + 1) * chunk]
        o_ref[:, c * chunk:(c + 1) * chunk] = (1.0 / (1.0 + jnp.exp(-z))).astype(BF16)


def _proj_call(body, h, w, extra, extra_specs, n_out, name, *, tm):
    t, d = h.shape
    return pl.pallas_call(
        body,
        grid=(t // tm,),
        in_specs=[pl.BlockSpec((tm, d), lambda i: (i, 0)), _resident(w.shape)] + extra_specs,
        out_specs=pl.BlockSpec((tm, n_out), lambda i: (i, 0)),
        out_shape=jax.ShapeDtypeStruct((t, n_out), BF16),
        compiler_params=pltpu.CompilerParams(
            dimension_semantics=("parallel",), vmem_limit_bytes=VMEM_LIMIT),
        name=name,
    )(h, w, *extra)


def _attn_a_body(sink_ref, q_ref, kp_ref, kc_ref, kn_ref, vp_ref, vc_ref, vn_ref, o_ref, *, tq, nq):
    i = pl.program_id(1)
    nk = tq + 2 * WINDOW
    row = lax.broadcasted_iota(jnp.int32, (tq, nk), 0)
    col = lax.broadcasted_iota(jnp.int32, (tq, nk), 1)
    rel = col - WINDOW - row
    valid = (rel >= -WINDOW) & (rel <= WINDOW)
    valid = valid & ((col >= WINDOW) | (i > 0))
    valid = valid & ((col < tq + WINDOW) | (i < nq - 1))
    k_all = jnp.concatenate([kp_ref[0], kc_ref[0], kn_ref[0]], axis=0)
    v_all = jnp.concatenate([vp_ref[0], vc_ref[0], vn_ref[0]], axis=0)
    for g in range(A_KV_HEADS):
        k = k_all[:, g * HEAD_DIM:(g + 1) * HEAD_DIM]
        v = v_all[:, g * HEAD_DIM:(g + 1) * HEAD_DIM]
        for r in range(A_REP):
            hd = g * A_REP + r
            q = q_ref[0, :, hd * HEAD_DIM:(hd + 1) * HEAD_DIM]
            s = lax.dot_general(q, k, (((1,), (1,)), ((), ())), preferred_element_type=F32)
            s = jnp.where(valid, s, NEG_INF)
            sk = sink_ref[hd] * LOG2E
            m = jnp.maximum(jnp.max(s, axis=-1, keepdims=True), sk)
            e = jnp.exp2(s - m)
            den = jnp.sum(e, axis=-1, keepdims=True) + jnp.exp2(sk - m)
            o = jnp.dot(e.astype(BF16), v, preferred_element_type=F32) / den
            o_ref[0, :, hd * HEAD_DIM:(hd + 1) * HEAD_DIM] = o.astype(BF16)


def _attn_a(rope3, v3, sink, *, tq=256):
    b, s, _ = rope3.shape
    nq = s // tq
    nblk = s // WINDOW
    per = tq // WINDOW
    ka_blk = A_Q // A_KV
    prev_map = lambda bi, i: (bi, jnp.maximum(i * per - 1, 0), ka_blk)
    cur_map = lambda bi, i: (bi, i, ka_blk)
    next_map = lambda bi, i: (bi, jnp.minimum((i + 1) * per, nblk - 1), ka_blk)
    vprev_map = lambda bi, i: (bi, jnp.maximum(i * per - 1, 0), 0)
    vcur_map = lambda bi, i: (bi, i, 0)
    vnext_map = lambda bi, i: (bi, jnp.minimum((i + 1) * per, nblk - 1), 0)
    return pl.pallas_call(
        functools.partial(_attn_a_body, tq=tq, nq=nq),
        grid=(b, nq),
        in_specs=[
            pl.BlockSpec(memory_space=pltpu.SMEM),
            pl.BlockSpec((1, tq, A_Q), lambda bi, i: (bi, i, 0)),
            pl.BlockSpec((1, WINDOW, A_KV), prev_map),
            pl.BlockSpec((1, tq, A_KV), cur_map),
            pl.BlockSpec((1, WINDOW, A_KV), next_map),
            pl.BlockSpec((1, WINDOW, A_KV), vprev_map),
            pl.BlockSpec((1, tq, A_KV), vcur_map),
            pl.BlockSpec((1, WINDOW, A_KV), vnext_map),
        ],
        out_specs=pl.BlockSpec((1, tq, A_Q), lambda bi, i: (bi, i, 0)),
        out_shape=jax.ShapeDtypeStruct((b, s, A_Q), BF16),
        compiler_params=pltpu.CompilerParams(
            dimension_semantics=("parallel", "parallel"), vmem_limit_bytes=VMEM_LIMIT),
        name="attn_a",
    )(sink, rope3, rope3, rope3, rope3, v3, v3, v3)


def _attn_b_body(q_ref, k_ref, v_ref, lam_ref, g_ref, o_ref,
                 s_scr, mx_scr, p_scr, a_scr, m_scr, l_scr, acc_scr, *, tk, nk, lam_init):
    m_scr[...] = jnp.full_like(m_scr, NEG_INF)
    l_scr[...] = jnp.zeros_like(l_scr)
    acc_scr[...] = jnp.zeros_like(acc_scr)
    reps = tk // LANES

    def scores(c, slot):
        off = pl.multiple_of(c * tk, tk)
        kc = k_ref[0, pl.ds(off, tk), :]
        for comp in range(2):
            q = q_ref[0, :, comp * HEAD_DIM:(comp + 1) * HEAD_DIM]
            k = kc[:, comp * HEAD_DIM:(comp + 1) * HEAD_DIM]
            s = lax.dot_general(q, k, (((1,), (1,)), ((), ())), preferred_element_type=F32)
            s_scr[slot, comp] = s
            mx = s[:, :LANES]
            for j in range(1, reps):
                mx = jnp.maximum(mx, s[:, j * LANES:(j + 1) * LANES])
            mx_scr[slot, comp] = mx

    def softmax(slot):
        for comp in range(2):
            m_prev = m_scr[comp]
            m_new = jnp.maximum(m_prev, jnp.max(mx_scr[slot, comp], axis=-1, keepdims=True))
            a_scr[slot, comp] = jnp.exp2(m_prev - m_new)
            m_scr[comp] = m_new
            e = jnp.exp2(s_scr[slot, comp] - _lane_tile(m_new, reps))
            e_sum = e[:, :LANES]
            for j in range(1, reps):
                e_sum = e_sum + e[:, j * LANES:(j + 1) * LANES]
            l_scr[comp] = a_scr[slot, comp] * l_scr[comp] + e_sum
            p_scr[slot, comp] = e.astype(BF16)

    def pv(c, slot):
        off = pl.multiple_of(c * tk, tk)
        vc = v_ref[0, pl.ds(off, tk), :]
        for comp in range(2):
            acc_scr[comp] = _lane_tile(a_scr[slot, comp], 2) * acc_scr[comp] + jnp.dot(
                p_scr[slot, comp], vc, preferred_element_type=F32)

    def steady_pair(i):
        pv(i - 1, 0)
        scores(i + 1, 0)
        softmax(1)
        pv(i, 1)
        scores(i + 2, 1)
        softmax(0)

    def steady_quad(j, carry):
        steady_pair(4 * j + 1)
        steady_pair(4 * j + 3)
        return carry

    scores(0, 0)
    scores(1, 1)
    softmax(0)
    n_pairs = (nk - 2) // 2
    lax.fori_loop(0, n_pairs // 2, steady_quad, 0)
    if n_pairs % 2:
        steady_pair(nk - 3)
    softmax(1)
    pv(nk - 2, 0)
    pv(nk - 1, 1)

    lam_v = lam_ref[...]
    lam = (jnp.exp(jnp.sum(lam_v[0:1] * lam_v[1:2], axis=-1, keepdims=True))
           - jnp.exp(jnp.sum(lam_v[2:3] * lam_v[3:4], axis=-1, keepdims=True))
           + lam_init)
    l0 = jnp.sum(l_scr[0], axis=-1, keepdims=True)
    l1 = jnp.sum(l_scr[1], axis=-1, keepdims=True)
    o = acc_scr[0] / l0 - lam * (acc_scr[1] / l1)
    o_ref[0] = (_rms(o, g_ref[...]) * (1.0 - lam_init)).astype(BF16)


def _attn_b(rope3, v3, lam_vecs, subln_g, lam_init, *, tq=512, tk=512):
    b, s, _ = rope3.shape
    nk = s // tk
    assert nk % 2 == 0 and nk >= 4
    w = 2 * HEAD_DIM
    qb_blk = (A_Q + A_KV) // w
    kb_blk = qb_blk + B_HEADS
    vb_blk = A_KV // w
    return pl.pallas_call(
        functools.partial(_attn_b_body, tk=tk, nk=nk, lam_init=lam_init),
        grid=(b, B_HEADS, s // tq),
        in_specs=[
            pl.BlockSpec((1, tq, w), lambda bi, h, i: (bi, i, qb_blk + h)),
            pl.BlockSpec((1, s, w), lambda bi, h, i: (bi, 0, kb_blk + h)),
            pl.BlockSpec((1, s, w), lambda bi, h, i: (bi, 0, vb_blk + h)),
            pl.BlockSpec((4, HEAD_DIM), lambda bi, h, i: (0, 0)),
            pl.BlockSpec((1, w), lambda bi, h, i: (0, 0)),
        ],
        out_specs=pl.BlockSpec((1, tq, w), lambda bi, h, i: (bi, i, h)),
        out_shape=jax.ShapeDtypeStruct((b, s, B_V), BF16),
        scratch_shapes=[pltpu.VMEM((2, 2, tq, tk), F32),
                        pltpu.VMEM((2, 2, tq, LANES), F32),
                        pltpu.VMEM((2, 2, tq, tk), BF16),
                        pltpu.VMEM((2, 2, tq, LANES), F32),
                        pltpu.VMEM((2, tq, LANES), F32),
                        pltpu.VMEM((2, tq, LANES), F32),
                        pltpu.VMEM((2, tq, w), F32)],
        compiler_params=pltpu.CompilerParams(
            dimension_semantics=("parallel", "parallel", "arbitrary"),
            vmem_limit_bytes=VMEM_LIMIT),
        name="attn_b",
    )(rope3, rope3, v3, lam_vecs, subln_g)


def _merge_body(oa_ref, ob_ref, gate_ref, x_ref, wa_ref, wb_ref, wo_ref, g_ref, o_ref):
    pa = jnp.dot(oa_ref[...], wa_ref[...], preferred_element_type=F32)
    pb = jnp.dot(ob_ref[...], wb_ref[...], preferred_element_type=F32)
    ga = gate_ref[:, :D_MODEL].astype(F32)
    gb = gate_ref[:, D_MODEL:].astype(F32)
    merged = (ga * pa + gb * pb).astype(BF16)
    y = jnp.dot(merged, wo_ref[...], preferred_element_type=F32)
    o_ref[...] = x_ref[...] + _rms(y, g_ref[...])


def _merge(oa, ob, gates, x1, wa, wb, wo, g_post, *, tm=512):
    t, d = x1.shape
    rows = lambda n: pl.BlockSpec((tm, n), lambda i: (i, 0))
    return pl.pallas_call(
        _merge_body,
        grid=(t // tm,),
        in_specs=[rows(A_Q), rows(B_V), rows(GATE_COLS), rows(d),
                  _resident(wa.shape), _resident(wb.shape), _resident(wo.shape),
                  pl.BlockSpec((1, d), lambda i: (0, 0))],
        out_specs=rows(d),
        out_shape=jax.ShapeDtypeStruct((t, d), F32),
        compiler_params=pltpu.CompilerParams(
            dimension_semantics=("parallel",), vmem_limit_bytes=VMEM_LIMIT),
        name="merge",
    )(oa, ob, gates, x1, wa, wb, wo, g_post)


def _rope_tables(seq):
    pos = jnp.arange(seq, dtype=F32)
    inv = ROPE_THETA ** (-jnp.arange(0, HEAD_DIM, 2, dtype=F32) / HEAD_DIM)
    ang = pos[:, None] * inv[None, :]
    cos, sin = jnp.cos(ang), jnp.sin(ang)
    return jnp.concatenate([cos, cos], axis=-1), jnp.concatenate([-sin, sin], axis=-1)


def kernel(x, ffn1_pre_g, ffn1_w_gate, ffn1_w_up, ffn1_w_down, ffn1_post_g, mix_pre_g, w_in, gate_bias, sink_logit, lambda_q1, lambda_k1, lambda_q2, lambda_k2, subln_g, w_proj_a, w_proj_b, w_out, mix_post_g, ffn2_pre_g, ffn2_w_gate, ffn2_w_up, ffn2_w_down, ffn2_post_g):
    b, s, d = x.shape
    t = b * s
    depth = ffn1_pre_g.shape[0]
    cos2, sin2 = _rope_tables(s)
    xt = x.reshape(t, d)
    q_scaled = (QK_SCALE,) * A_Q_HEADS + (1.0,) * A_KV_HEADS + (QK_SCALE,) * (2 * B_HEADS) + (1.0,) * (2 * B_HEADS)
    tm_proj = 512
    for l in range(depth):
        lam_init = 0.8 - 0.6 * math.exp(-0.3 * l)
        row = lambda p: p[l:l + 1]
        bf = lambda p: p[l].astype(BF16)

        xt, h = _ffn(xt, row(ffn1_pre_g), bf(ffn1_w_gate), bf(ffn1_w_up), bf(ffn1_w_down),
                     row(ffn1_post_g), row(mix_pre_g))

        w = w_in[l]
        qa_end, ka_end, va_end = A_Q, A_Q + A_KV, A_Q + 2 * A_KV
        qb_end = va_end + B_QK
        kb_end = qb_end + B_QK
        vb_end = kb_end + B_V
        w_rope = jnp.concatenate([w[:, :ka_end], w[:, va_end:kb_end]], axis=1).astype(BF16)
        w_v = jnp.concatenate([w[:, ka_end:va_end], w[:, kb_end:vb_end]], axis=1).astype(BF16)
        w_gate = w[:, vb_end:].astype(BF16)

        ns = s // tm_proj
        tab = pl.BlockSpec((tm_proj, HEAD_DIM), lambda i: (i % ns, 0))
        rope = _proj_call(functools.partial(_proj_rope_body, head_scales=q_scaled),
                          h, w_rope, [cos2, sin2], [tab, tab], ROPE_COLS, "proj_rope", tm=tm_proj)
        vals = _proj_call(functools.partial(_proj_plain_body, chunk=MXU_N),
                          h, w_v, [], [], V_COLS, "proj_v", tm=tm_proj)
        gates = _proj_call(functools.partial(_proj_gate_body, chunk=2 * MXU_N),
                           h, w_gate, [gate_bias[l:l + 1]], [_resident((1, GATE_COLS))],
                           GATE_COLS, "proj_gate", tm=tm_proj)

        rope3 = rope.reshape(b, s, ROPE_COLS)
        v3 = vals.reshape(b, s, V_COLS)
        out_a = _attn_a(rope3, v3, sink_logit[l])
        lam_vecs = jnp.stack([lambda_q1[l], lambda_k1[l], lambda_q2[l], lambda_k2[l]]).astype(F32)
        out_b = _attn_b(rope3, v3, lam_vecs, row(subln_g), lam_init)

        xt = _merge(out_a.reshape(t, A_Q), out_b.reshape(t, B_V), gates, xt,
                    bf(w_proj_a), bf(w_proj_b), bf(w_out), row(mix_post_g))

        xt = _ffn(xt, row(ffn2_pre_g), bf(ffn2_w_gate), bf(ffn2_w_up), bf(ffn2_w_down),
                  row(ffn2_post_g))
    return xt.reshape(b, s, d)
```

```python
import functools
import math

import jax
import jax.numpy as jnp
from jax import lax
from jax.experimental import pallas as pl
from jax.experimental.pallas import tpu as pltpu

D_MODEL = 2048
HEAD_DIM = 128
A_Q_HEADS = 8
A_KV_HEADS = 2
A_REP = A_Q_HEADS // A_KV_HEADS
WINDOW = 128
B_HEADS = 4
D_FF = 5632
ROPE_THETA = 10000.0
EPS = 1e-6
NEG_INF = -1e30
LOG2E = 1.4426950408889634
QK_SCALE = HEAD_DIM ** -0.5 * LOG2E
FIXED_SHIFT_SPAN = 60.0

A_Q = A_Q_HEADS * HEAD_DIM
A_KV = A_KV_HEADS * HEAD_DIM
B_QK = B_HEADS * 2 * HEAD_DIM
B_V = B_HEADS * 2 * HEAD_DIM
ROPE_COLS = A_Q + A_KV + 2 * B_QK
V_COLS = A_KV + B_V
GATE_COLS = 2 * D_MODEL

LANES = 128
MXU_N = 256
VMEM_LIMIT = 56 * 1024 * 1024

F32 = jnp.float32
BF16 = jnp.bfloat16


def _rms(x, g):
    ms = jnp.mean(x * x, axis=-1, keepdims=True)
    return x * lax.rsqrt(ms + EPS) * g


def _lane_tile(x, n):
    return jnp.concatenate([x] * n, axis=1)


def _resident(shape):
    return pl.BlockSpec(shape, lambda *_: (0,) * len(shape), pipeline_mode=pl.Buffered(1))


def _ffn_body(*refs, nf, tf, n_sub_last, emit_next):
    if emit_next:
        (x_ref, gpre_ref, wg_ref, wu_ref, wd_ref, gpost_ref, gnext_ref,
         o_ref, hn_ref, h_scr) = refs
    else:
        (x_ref, gpre_ref, wg_ref, wu_ref, wd_ref, gpost_ref, o_ref, h_scr) = refs
    f = pl.program_id(1)
    n_sub = wg_ref.shape[1] // tf

    def sub_blocks(n, h, first=False):
        for j in range(n):
            cols = slice(j * tf, (j + 1) * tf)
            gate = jnp.dot(h, wg_ref[:, cols], preferred_element_type=F32)
            up = jnp.dot(h, wu_ref[:, cols], preferred_element_type=F32)
            a = gate * (1.0 / (1.0 + jnp.exp(-gate))) * up
            y = jnp.dot(a.astype(BF16), wd_ref[cols, :], preferred_element_type=F32)
            if first and j == 0:
                o_ref[...] = y
            else:
                o_ref[...] += y

    @pl.when(f == 0)
    def _():
        h = _rms(x_ref[...], gpre_ref[...]).astype(BF16)
        h_scr[...] = h
        sub_blocks(n_sub, h, first=True)

    @pl.when((f > 0) & (f < nf - 1))
    def _():
        sub_blocks(n_sub, h_scr[...])

    @pl.when(f == nf - 1)
    def _():
        sub_blocks(n_sub_last, h_scr[...])
        xo = x_ref[...] + 0.5 * _rms(o_ref[...], gpost_ref[...])
        o_ref[...] = xo
        if emit_next:
            hn_ref[...] = _rms(xo, gnext_ref[...]).astype(BF16)


def _ffn(x, g_pre, wg, wu, wd, g_post, g_next=None, *, tm=512, tf=512, n_sub=2):
    t, d = x.shape
    blk = tf * n_sub
    nf = pl.cdiv(D_FF, blk)
    n_sub_last = (D_FF - (nf - 1) * blk) // tf
    assert (nf - 1) * blk + n_sub_last * tf == D_FF
    emit_next = g_next is not None
    row = pl.BlockSpec((tm, d), lambda i, f: (i, 0))
    vec = pl.BlockSpec((1, d), lambda i, f: (0, 0))
    in_specs = [row, vec,
                pl.BlockSpec((d, blk), lambda i, f: (0, f)),
                pl.BlockSpec((d, blk), lambda i, f: (0, f)),
                pl.BlockSpec((blk, d), lambda i, f: (f, 0)),
                vec]
    args = [x, g_pre, wg, wu, wd, g_post]
    out_shape = [jax.ShapeDtypeStruct((t, d), F32)]
    out_specs = [row]
    if emit_next:
        in_specs.append(vec)
        args.append(g_next)
        out_shape.append(jax.ShapeDtypeStruct((t, d), BF16))
        out_specs.append(row)
    res = pl.pallas_call(
        functools.partial(_ffn_body, nf=nf, tf=tf, n_sub_last=n_sub_last, emit_next=emit_next),
        grid=(t // tm, nf),
        in_specs=in_specs,
        out_specs=out_specs,
        out_shape=out_shape,
        scratch_shapes=[pltpu.VMEM((tm, d), BF16)],
        compiler_params=pltpu.CompilerParams(
            dimension_semantics=("parallel", "arbitrary"), vmem_limit_bytes=VMEM_LIMIT),
        name="ffn_next" if emit_next else "ffn",
    )(*args)
    return res if emit_next else res[0]


def _proj_rope_body(h_ref, w_ref, cos_ref, sin_ref, o_ref, *, head_scales):
    h = h_ref[...]
    cos = cos_ref[...]
    sin = sin_ref[...]
    heads_per_chunk = MXU_N // HEAD_DIM
    for c in range(len(head_scales) // heads_per_chunk):
        acc = jnp.dot(h, w_ref[:, c * MXU_N:(c + 1) * MXU_N], preferred_element_type=F32)
        for k in range(heads_per_chunk):
            hd = c * heads_per_chunk + k
            a = acc[:, k * HEAD_DIM:(k + 1) * HEAD_DIM]
            r = a * cos + pltpu.roll(a, HEAD_DIM // 2, 1) * sin
            if head_scales[hd] != 1.0:
                r = r * head_scales[hd]
            o_ref[:, hd * HEAD_DIM:(hd + 1) * HEAD_DIM] = r.astype(BF16)


def _proj_plain_body(h_ref, w_ref, o_ref, *, chunk):
    h = h_ref[...]
    for c in range(w_ref.shape[1] // chunk):
        acc = jnp.dot(h, w_ref[:, c * chunk:(c + 1) * chunk], preferred_element_type=F32)
        o_ref[:, c * chunk:(c + 1) * chunk] = acc.astype(BF16)


def _proj_gate_body(h_ref, w_ref, b_ref, o_ref, *, chunk):
    h = h_ref[...]
    for c in range(w_ref.shape[1] // chunk):
        acc = jnp.dot(h, w_ref[:, c * chunk:(c + 1) * chunk], preferred_element_type=F32)
        z = acc + b_ref[:, c * chunk:(c + 1) * chunk]
        o_ref[:, c * chunk:(c + 1) * chunk] = (1.0 / (1.0 + jnp.exp(-z))).astype(BF16)


def _proj_call(body, h, w, extra, extra_specs, n_out, name, *, tm):
    t, d = h.shape
    return pl.pallas_call(
        body,
        grid=(t // tm,),
        in_specs=[pl.BlockSpec((tm, d), lambda i: (i, 0)), _resident(w.shape)] + extra_specs,
        out_specs=pl.BlockSpec((tm, n_out), lambda i: (i, 0)),
        out_shape=jax.ShapeDtypeStruct((t, n_out), BF16),
        compiler_params=pltpu.CompilerParams(
            dimension_semantics=("parallel",), vmem_limit_bytes=VMEM_LIMIT),
        name=name,
    )(h, w, *extra)


def _attn_a_body(sink_ref, q_ref, kp_ref, kc_ref, kn_ref, vp_ref, vc_ref, vn_ref, o_ref, *, tq, nq):
    i = pl.program_id(1)
    nk = tq + 2 * WINDOW
    row = lax.broadcasted_iota(jnp.int32, (tq, nk), 0)
    col = lax.broadcasted_iota(jnp.int32, (tq, nk), 1)
    rel = col - WINDOW - row
    valid = (rel >= -WINDOW) & (rel <= WINDOW)
    valid = valid & ((col >= WINDOW) | (i > 0))
    valid = valid & ((col < tq + WINDOW) | (i < nq - 1))
    k_all = jnp.concatenate([kp_ref[0], kc_ref[0], kn_ref[0]], axis=0)
    v_all = jnp.concatenate([vp_ref[0], vc_ref[0], vn_ref[0]], axis=0)
    for g in range(A_KV_HEADS):
        k = k_all[:, g * HEAD_DIM:(g + 1) * HEAD_DIM]
        v = v_all[:, g * HEAD_DIM:(g + 1) * HEAD_DIM]
        for r in range(A_REP):
            hd = g * A_REP + r
            q = q_ref[0, :, hd * HEAD_DIM:(hd + 1) * HEAD_DIM]
            s = lax.dot_general(q, k, (((1,), (1,)), ((), ())), preferred_element_type=F32)
            s = jnp.where(valid, s, NEG_INF)
            sk = sink_ref[hd] * LOG2E
            m = jnp.maximum(jnp.max(s, axis=-1, keepdims=True), sk)
            e = jnp.exp2(s - m)
            den = jnp.sum(e, axis=-1, keepdims=True) + jnp.exp2(sk - m)
            o = jnp.dot(e.astype(BF16), v, preferred_element_type=F32) / den
            o_ref[0, :, hd * HEAD_DIM:(hd + 1) * HEAD_DIM] = o.astype(BF16)


def _attn_a(rope3, v3, sink, *, tq=256):
    b, s, _ = rope3.shape
    nq = s // tq
    nblk = s // WINDOW
    per = tq // WINDOW
    ka_blk = A_Q // A_KV
    prev_map = lambda bi, i: (bi, jnp.maximum(i * per - 1, 0), ka_blk)
    cur_map = lambda bi, i: (bi, i, ka_blk)
    next_map = lambda bi, i: (bi, jnp.minimum((i + 1) * per, nblk - 1), ka_blk)
    vprev_map = lambda bi, i: (bi, jnp.maximum(i * per - 1, 0), 0)
    vcur_map = lambda bi, i: (bi, i, 0)
    vnext_map = lambda bi, i: (bi, jnp.minimum((i + 1) * per, nblk - 1), 0)
    return pl.pallas_call(
        functools.partial(_attn_a_body, tq=tq, nq=nq),
        grid=(b, nq),
        in_specs=[
            pl.BlockSpec(memory_space=pltpu.SMEM),
            pl.BlockSpec((1, tq, A_Q), lambda bi, i: (bi, i, 0)),
            pl.BlockSpec((1, WINDOW, A_KV), prev_map),
            pl.BlockSpec((1, tq, A_KV), cur_map),
            pl.BlockSpec((1, WINDOW, A_KV), next_map),
            pl.BlockSpec((1, WINDOW, A_KV), vprev_map),
            pl.BlockSpec((1, tq, A_KV), vcur_map),
            pl.BlockSpec((1, WINDOW, A_KV), vnext_map),
        ],
        out_specs=pl.BlockSpec((1, tq, A_Q), lambda bi, i: (bi, i, 0)),
        out_shape=jax.ShapeDtypeStruct((b, s, A_Q), BF16),
        compiler_params=pltpu.CompilerParams(
            dimension_semantics=("parallel", "parallel"), vmem_limit_bytes=VMEM_LIMIT),
        name="attn_a",
    )(sink, rope3, rope3, rope3, rope3, v3, v3, v3)


def _attn_b_body(q_ref, k_ref, v_ref, lam_ref, g_ref, o_ref,
                 s_scr, mx_scr, p_scr, a_scr, m_scr, l_scr, acc_scr, kn_scr, *, tk, nk, lam_init):
    reps = tk // LANES
    tq = q_ref.shape[1]

    def key_chunk(c):
        return k_ref[0, pl.ds(pl.multiple_of(c * tk, tk), tk), :]

    def val_chunk(c):
        return v_ref[0, pl.ds(pl.multiple_of(c * tk, tk), tk), :]

    def qk(c, comp):
        q = q_ref[0, :, comp * HEAD_DIM:(comp + 1) * HEAD_DIM]
        k = key_chunk(c)[:, comp * HEAD_DIM:(comp + 1) * HEAD_DIM]
        return lax.dot_general(q, k, (((1,), (1,)), ((), ())), preferred_element_type=F32)

    def lane_fold(x, op):
        y = x[:, :LANES]
        for j in range(1, reps):
            y = op(y, x[:, j * LANES:(j + 1) * LANES])
        return y

    @pl.when(pl.program_id(2) == 0)
    def _():
        def chunk_max(c, carry):
            kc = key_chunk(c).astype(F32)
            sq = kc * kc
            out = []
            for comp in range(2):
                n2 = jnp.sum(sq[:, comp * HEAD_DIM:(comp + 1) * HEAD_DIM], axis=-1, keepdims=True)
                out.append(jnp.maximum(carry[comp], jnp.max(n2, axis=0, keepdims=True)))
            return tuple(out)
        zero = jnp.zeros((1, 1), F32)
        n2max = lax.fori_loop(0, nk, chunk_max, (zero, zero))
        for comp in range(2):
            kn_scr[comp] = jnp.broadcast_to(jnp.sqrt(n2max[comp]), kn_scr.shape[1:])

    def scores(c, slot):
        for comp in range(2):
            s = qk(c, comp)
            s_scr[slot, comp] = s
            mx_scr[slot, comp] = lane_fold(s, jnp.maximum)

    def softmax(slot):
        for comp in range(2):
            m_prev = m_scr[comp]
            m_new = jnp.maximum(m_prev, jnp.max(mx_scr[slot, comp], axis=-1, keepdims=True))
            a_scr[slot, comp] = jnp.exp2(m_prev - m_new)
            m_scr[comp] = m_new
            e = jnp.exp2(s_scr[slot, comp] - _lane_tile(m_new, reps))
            l_scr[comp] = a_scr[slot, comp] * l_scr[comp] + lane_fold(e, jnp.add)
            p_scr[slot, comp] = e.astype(BF16)

    def pv(c, slot):
        vc = val_chunk(c)
        for comp in range(2):
            acc_scr[comp] = _lane_tile(a_scr[slot, comp], 2) * acc_scr[comp] + jnp.dot(
                p_scr[slot, comp], vc, preferred_element_type=F32)

    def steady_pair(i):
        pv(i - 1, 0)
        scores(i + 1, 0)
        softmax(1)
        pv(i, 1)
        scores(i + 2, 1)
        softmax(0)

    def steady_quad(j, carry):
        steady_pair(4 * j + 1)
        steady_pair(4 * j + 3)
        return carry

    def probs(c, slot, s_of):
        for comp in range(2):
            e = jnp.exp2(s_of(comp) - _lane_tile(m_scr[comp], reps))
            l_scr[comp] += lane_fold(e, jnp.add)
            p_scr[slot, comp] = e.astype(BF16)

    def pv_plain(c, slot):
        vc = val_chunk(c)
        for comp in range(2):
            acc_scr[comp] += jnp.dot(p_scr[slot, comp], vc, preferred_element_type=F32)

    def fixed_pair(j, carry):
        c = 2 * j + 1
        pv_plain(c - 1, 0)
        probs(c, 1, lambda comp: qk(c, comp))
        pv_plain(c, 1)
        probs(c + 1, 0, lambda comp: qk(c + 1, comp))
        return carry

    scores(0, 0)
    span = jnp.zeros((1, 1), F32)
    for comp in range(2):
        qf = q_ref[0, :, comp * HEAD_DIM:(comp + 1) * HEAD_DIM].astype(F32)
        q_norm = jnp.sqrt(jnp.sum(qf * qf, axis=-1, keepdims=True))
        bound = q_norm * kn_scr[comp][:1, :1]
        first_max = jnp.max(mx_scr[0, comp], axis=-1, keepdims=True)
        span = jnp.maximum(span, jnp.max(bound - first_max, axis=0, keepdims=True))
    use_fixed = span[0, 0] <= FIXED_SHIFT_SPAN

    l_scr[...] = jnp.zeros_like(l_scr)
    acc_scr[...] = jnp.zeros_like(acc_scr)

    @pl.when(use_fixed)
    def _():
        for comp in range(2):
            m_scr[comp] = jnp.broadcast_to(
                jnp.max(mx_scr[0, comp], axis=-1, keepdims=True), (tq, LANES))
        probs(0, 0, lambda comp: s_scr[0, comp])
        lax.fori_loop(0, nk // 2 - 1, fixed_pair, 0)
        pv_plain(nk - 2, 0)
        probs(nk - 1, 1, lambda comp: qk(nk - 1, comp))
        pv_plain(nk - 1, 1)

    @pl.when(jnp.logical_not(use_fixed))
    def _():
        m_scr[...] = jnp.full_like(m_scr, NEG_INF)
        scores(1, 1)
        softmax(0)
        n_pairs = (nk - 2) // 2
        lax.fori_loop(0, n_pairs // 2, steady_quad, 0)
        if n_pairs % 2:
            steady_pair(nk - 3)
        softmax(1)
        pv(nk - 2, 0)
        pv(nk - 1, 1)

    lam_v = lam_ref[...]
    lam = (jnp.exp(jnp.sum(lam_v[0:1] * lam_v[1:2], axis=-1, keepdims=True))
           - jnp.exp(jnp.sum(lam_v[2:3] * lam_v[3:4], axis=-1, keepdims=True))
           + lam_init)
    l0 = jnp.sum(l_scr[0], axis=-1, keepdims=True)
    l1 = jnp.sum(l_scr[1], axis=-1, keepdims=True)
    o = acc_scr[0] / l0 - lam * (acc_scr[1] / l1)
    o_ref[0] = (_rms(o, g_ref[...]) * (1.0 - lam_init)).astype(BF16)


def _attn_b(rope3, v3, lam_vecs, subln_g, lam_init, *, tq=512, tk=512):
    b, s, _ = rope3.shape
    nk = s // tk
    assert nk % 2 == 0 and nk >= 4
    w = 2 * HEAD_DIM
    qb_blk = (A_Q + A_KV) // w
    kb_blk = qb_blk + B_HEADS
    vb_blk = A_KV // w
    return pl.pallas_call(
        functools.partial(_attn_b_body, tk=tk, nk=nk, lam_init=lam_init),
        grid=(b, B_HEADS, s // tq),
        in_specs=[
            pl.BlockSpec((1, tq, w), lambda bi, h, i: (bi, i, qb_blk + h)),
            pl.BlockSpec((1, s, w), lambda bi, h, i: (bi, 0, kb_blk + h)),
            pl.BlockSpec((1, s, w), lambda bi, h, i: (bi, 0, vb_blk + h)),
            pl.BlockSpec((4, HEAD_DIM), lambda bi, h, i: (0, 0)),
            pl.BlockSpec((1, w), lambda bi, h, i: (0, 0)),
        ],
        out_specs=pl.BlockSpec((1, tq, w), lambda bi, h, i: (bi, i, h)),
        out_shape=jax.ShapeDtypeStruct((b, s, B_V), BF16),
        scratch_shapes=[pltpu.VMEM((2, 2, tq, tk), F32),
                        pltpu.VMEM((2, 2, tq, LANES), F32),
                        pltpu.VMEM((2, 2, tq, tk), BF16),
                        pltpu.VMEM((2, 2, tq, LANES), F32),
                        pltpu.VMEM((2, tq, LANES), F32),
                        pltpu.VMEM((2, tq, LANES), F32),
                        pltpu.VMEM((2, tq, w), F32),
                        pltpu.VMEM((2, 8, LANES), F32)],
        compiler_params=pltpu.CompilerParams(
            dimension_semantics=("parallel", "parallel", "arbitrary"),
            vmem_limit_bytes=VMEM_LIMIT),
        name="attn_b",
    )(rope3, rope3, v3, lam_vecs, subln_g)


def _merge_body(oa_ref, ob_ref, gate_ref, x_ref, wa_ref, wb_ref, wo_ref, g_ref, o_ref):
    pa = jnp.dot(oa_ref[...], wa_ref[...], preferred_element_type=F32)
    pb = jnp.dot(ob_ref[...], wb_ref[...], preferred_element_type=F32)
    ga = gate_ref[:, :D_MODEL].astype(F32)
    gb = gate_ref[:, D_MODEL:].astype(F32)
    merged = (ga * pa + gb * pb).astype(BF16)
    y = jnp.dot(merged, wo_ref[...], preferred_element_type=F32)
    o_ref[...] = x_ref[...] + _rms(y, g_ref[...])


def _merge(oa, ob, gates, x1, wa, wb, wo, g_post, *, tm=512):
    t, d = x1.shape
    rows = lambda n: pl.BlockSpec((tm, n), lambda i: (i, 0))
    return pl.pallas_call(
        _merge_body,
        grid=(t // tm,),
        in_specs=[rows(A_Q), rows(B_V), rows(GATE_COLS), rows(d),
                  _resident(wa.shape), _resident(wb.shape), _resident(wo.shape),
                  pl.BlockSpec((1, d), lambda i: (0, 0))],
        out_specs=rows(d),
        out_shape=jax.ShapeDtypeStruct((t, d), F32),
        compiler_params=pltpu.CompilerParams(
            dimension_semantics=("parallel",), vmem_limit_bytes=VMEM_LIMIT),
        name="merge",
    )(oa, ob, gates, x1, wa, wb, wo, g_post)


def _rope_tables(seq):
    pos = jnp.arange(seq, dtype=F32)
    inv = ROPE_THETA ** (-jnp.arange(0, HEAD_DIM, 2, dtype=F32) / HEAD_DIM)
    ang = pos[:, None] * inv[None, :]
    cos, sin = jnp.cos(ang), jnp.sin(ang)
    return jnp.concatenate([cos, cos], axis=-1), jnp.concatenate([-sin, sin], axis=-1)


def kernel(x, ffn1_pre_g, ffn1_w_gate, ffn1_w_up, ffn1_w_down, ffn1_post_g, mix_pre_g, w_in, gate_bias, sink_logit, lambda_q1, lambda_k1, lambda_q2, lambda_k2, subln_g, w_proj_a, w_proj_b, w_out, mix_post_g, ffn2_pre_g, ffn2_w_gate, ffn2_w_up, ffn2_w_down, ffn2_post_g):
    b, s, d = x.shape
    t = b * s
    depth = ffn1_pre_g.shape[0]
    cos2, sin2 = _rope_tables(s)
    xt = x.reshape(t, d)
    q_scaled = (QK_SCALE,) * A_Q_HEADS + (1.0,) * A_KV_HEADS + (QK_SCALE,) * (2 * B_HEADS) + (1.0,) * (2 * B_HEADS)
    tm_proj = 512
    for l in range(depth):
        lam_init = 0.8 - 0.6 * math.exp(-0.3 * l)
        row = lambda p: p[l:l + 1]
        bf = lambda p: p[l].astype(BF16)

        xt, h = _ffn(xt, row(ffn1_pre_g), bf(ffn1_w_gate), bf(ffn1_w_up), bf(ffn1_w_down),
                     row(ffn1_post_g), row(mix_pre_g))

        w = w_in[l]
        qa_end, ka_end, va_end = A_Q, A_Q + A_KV, A_Q + 2 * A_KV
        qb_end = va_end + B_QK
        kb_end = qb_end + B_QK
        vb_end = kb_end + B_V
        w_rope = jnp.concatenate([w[:, :ka_end], w[:, va_end:kb_end]], axis=1).astype(BF16)
        w_v = jnp.concatenate([w[:, ka_end:va_end], w[:, kb_end:vb_end]], axis=1).astype(BF16)
        w_gate = w[:, vb_end:].astype(BF16)

        ns = s // tm_proj
        tab = pl.BlockSpec((tm_proj, HEAD_DIM), lambda i: (i % ns, 0))
        rope = _proj_call(functools.partial(_proj_rope_body, head_scales=q_scaled),
                          h, w_rope, [cos2, sin2], [tab, tab], ROPE_COLS, "proj_rope", tm=tm_proj)
        vals = _proj_call(functools.partial(_proj_plain_body, chunk=MXU_N),
                          h, w_v, [], [], V_COLS, "proj_v", tm=tm_proj)
        gates = _proj_call(functools.partial(_proj_gate_body, chunk=2 * MXU_N),
                           h, w_gate, [gate_bias[l:l + 1]], [_resident((1, GATE_COLS))],
                           GATE_COLS, "proj_gate", tm=tm_proj)

        rope3 = rope.reshape(b, s, ROPE_COLS)
        v3 = vals.reshape(b, s, V_COLS)
        out_a = _attn_a(rope3, v3, sink_logit[l])
        lam_vecs = jnp.stack([lambda_q1[l], lambda_k1[l], lambda_q2[l], lambda_k2[l]]).astype(F32)
        out_b = _attn_b(rope3, v3, lam_vecs, row(subln_g), lam_init)

        xt = _merge(out_a.reshape(t, A_Q), out_b.reshape(t, B_V), gates, xt,
                    bf(w_proj_a), bf(w_proj_b), bf(w_out), row(mix_post_g))

        xt = _ffn(xt, row(ffn2_pre_g), bf(ffn2_w_gate), bf(ffn2_w_up), bf(ffn2_w_down),
                  row(ffn2_post_g))
    return xt.reshape(b, s, d)
```

```python
import functools
import math

import jax
import jax.numpy as jnp
from jax import lax
from jax.experimental import pallas as pl
from jax.experimental.pallas import tpu as pltpu

D_MODEL = 2048
HEAD_DIM = 128
A_Q_HEADS = 8
A_KV_HEADS = 2
A_REP = A_Q_HEADS // A_KV_HEADS
WINDOW = 128
B_HEADS = 4
D_FF = 5632
ROPE_THETA = 10000.0
EPS = 1e-6
NEG_INF = -1e30
LOG2E = 1.4426950408889634
QK_SCALE = HEAD_DIM ** -0.5 * LOG2E
FIXED_SHIFT_SPAN = 60.0
FIXED_GROUP = 6

A_Q = A_Q_HEADS * HEAD_DIM
A_KV = A_KV_HEADS * HEAD_DIM
B_QK = B_HEADS * 2 * HEAD_DIM
B_V = B_HEADS * 2 * HEAD_DIM
ROPE_COLS = A_Q + A_KV + 2 * B_QK
V_COLS = A_KV + B_V
GATE_COLS = 2 * D_MODEL

LANES = 128
MXU_N = 256
VMEM_LIMIT = 56 * 1024 * 1024

F32 = jnp.float32
BF16 = jnp.bfloat16


def _rms(x, g):
    ms = jnp.mean(x * x, axis=-1, keepdims=True)
    return x * lax.rsqrt(ms + EPS) * g


def _lane_tile(x, n):
    return jnp.concatenate([x] * n, axis=1)


def _resident(shape):
    return pl.BlockSpec(shape, lambda *_: (0,) * len(shape), pipeline_mode=pl.Buffered(1))


def _ffn_body(*refs, nf, tf, n_sub_last, emit_next):
    if emit_next:
        (x_ref, gpre_ref, wg_ref, wu_ref, wd_ref, gpost_ref, gnext_ref,
         o_ref, hn_ref, h_scr) = refs
    else:
        (x_ref, gpre_ref, wg_ref, wu_ref, wd_ref, gpost_ref, o_ref, h_scr) = refs
    f = pl.program_id(1)
    n_sub = wg_ref.shape[1] // tf

    def sub_blocks(n, h, first=False):
        for j in range(n):
            cols = slice(j * tf, (j + 1) * tf)
            gate = jnp.dot(h, wg_ref[:, cols], preferred_element_type=F32)
            up = jnp.dot(h, wu_ref[:, cols], preferred_element_type=F32)
            a = gate * (1.0 / (1.0 + jnp.exp(-gate))) * up
            y = jnp.dot(a.astype(BF16), wd_ref[cols, :], preferred_element_type=F32)
            if first and j == 0:
                o_ref[...] = y
            else:
                o_ref[...] += y

    @pl.when(f == 0)
    def _():
        h = _rms(x_ref[...], gpre_ref[...]).astype(BF16)
        h_scr[...] = h
        sub_blocks(n_sub, h, first=True)

    @pl.when((f > 0) & (f < nf - 1))
    def _():
        sub_blocks(n_sub, h_scr[...])

    @pl.when(f == nf - 1)
    def _():
        sub_blocks(n_sub_last, h_scr[...])
        xo = x_ref[...] + 0.5 * _rms(o_ref[...], gpost_ref[...])
        o_ref[...] = xo
        if emit_next:
            hn_ref[...] = _rms(xo, gnext_ref[...]).astype(BF16)


def _ffn(x, g_pre, wg, wu, wd, g_post, g_next=None, *, tm=512, tf=512, n_sub=2):
    t, d = x.shape
    blk = tf * n_sub
    nf = pl.cdiv(D_FF, blk)
    n_sub_last = (D_FF - (nf - 1) * blk) // tf
    assert (nf - 1) * blk + n_sub_last * tf == D_FF
    emit_next = g_next is not None
    row = pl.BlockSpec((tm, d), lambda i, f: (i, 0))
    vec = pl.BlockSpec((1, d), lambda i, f: (0, 0))
    in_specs = [row, vec,
                pl.BlockSpec((d, blk), lambda i, f: (0, f)),
                pl.BlockSpec((d, blk), lambda i, f: (0, f)),
                pl.BlockSpec((blk, d), lambda i, f: (f, 0)),
                vec]
    args = [x, g_pre, wg, wu, wd, g_post]
    out_shape = [jax.ShapeDtypeStruct((t, d), F32)]
    out_specs = [row]
    if emit_next:
        in_specs.append(vec)
        args.append(g_next)
        out_shape.append(jax.ShapeDtypeStruct((t, d), BF16))
        out_specs.append(row)
    res = pl.pallas_call(
        functools.partial(_ffn_body, nf=nf, tf=tf, n_sub_last=n_sub_last, emit_next=emit_next),
        grid=(t // tm, nf),
        in_specs=in_specs,
        out_specs=out_specs,
        out_shape=out_shape,
        scratch_shapes=[pltpu.VMEM((tm, d), BF16)],
        compiler_params=pltpu.CompilerParams(
            dimension_semantics=("parallel", "arbitrary"), vmem_limit_bytes=VMEM_LIMIT),
        name="ffn_next" if emit_next else "ffn",
    )(*args)
    return res if emit_next else res[0]


def _proj_rope_body(h_ref, w_ref, cos_ref, sin_ref, o_ref, *, head_scales):
    h = h_ref[...]
    cos = cos_ref[...]
    sin = sin_ref[...]
    heads_per_chunk = MXU_N // HEAD_DIM
    for c in range(len(head_scales) // heads_per_chunk):
        acc = jnp.dot(h, w_ref[:, c * MXU_N:(c + 1) * MXU_N], preferred_element_type=F32)
        for k in range(heads_per_chunk):
            hd = c * heads_per_chunk + k
            a = acc[:, k * HEAD_DIM:(k + 1) * HEAD_DIM]
            r = a * cos + pltpu.roll(a, HEAD_DIM // 2, 1) * sin
            if head_scales[hd] != 1.0:
                r = r * head_scales[hd]
            o_ref[:, hd * HEAD_DIM:(hd + 1) * HEAD_DIM] = r.astype(BF16)


def _proj_plain_body(h_ref, w_ref, o_ref, *, chunk):
    h = h_ref[...]
    for c in range(w_ref.shape[1] // chunk):
        acc = jnp.dot(h, w_ref[:, c * chunk:(c + 1) * chunk], preferred_element_type=F32)
        o_ref[:, c * chunk:(c + 1) * chunk] = acc.astype(BF16)


def _proj_gate_body(h_ref, w_ref, b_ref, o_ref, *, chunk):
    h = h_ref[...]
    for c in range(w_ref.shape[1] // chunk):
        acc = jnp.dot(h, w_ref[:, c * chunk:(c + 1) * chunk], preferred_element_type=F32)
        z = acc + b_ref[:, c * chunk:(c + 1) * chunk]
        o_ref[:, c * chunk:(c + 1) * chunk] = (1.0 / (1.0 + jnp.exp(-z))).astype(BF16)


def _proj_call(body, h, w, extra, extra_specs, n_out, name, *, tm):
    t, d = h.shape
    return pl.pallas_call(
        body,
        grid=(t // tm,),
        in_specs=[pl.BlockSpec((tm, d), lambda i: (i, 0)), _resident(w.shape)] + extra_specs,
        out_specs=pl.BlockSpec((tm, n_out), lambda i: (i, 0)),
        out_shape=jax.ShapeDtypeStruct((t, n_out), BF16),
        compiler_params=pltpu.CompilerParams(
            dimension_semantics=("parallel",), vmem_limit_bytes=VMEM_LIMIT),
        name=name,
    )(h, w, *extra)


def _attn_a_body(sink_ref, q_ref, kp_ref, kc_ref, kn_ref, vp_ref, vc_ref, vn_ref, o_ref, *, tq, nq):
    i = pl.program_id(1)
    nk = tq + 2 * WINDOW
    row = lax.broadcasted_iota(jnp.int32, (tq, nk), 0)
    col = lax.broadcasted_iota(jnp.int32, (tq, nk), 1)
    rel = col - WINDOW - row
    valid = (rel >= -WINDOW) & (rel <= WINDOW)
    valid = valid & ((col >= WINDOW) | (i > 0))
    valid = valid & ((col < tq + WINDOW) | (i < nq - 1))
    k_all = jnp.concatenate([kp_ref[0], kc_ref[0], kn_ref[0]], axis=0)
    v_all = jnp.concatenate([vp_ref[0], vc_ref[0], vn_ref[0]], axis=0)
    for g in range(A_KV_HEADS):
        k = k_all[:, g * HEAD_DIM:(g + 1) * HEAD_DIM]
        v = v_all[:, g * HEAD_DIM:(g + 1) * HEAD_DIM]
        for r in range(A_REP):
            hd = g * A_REP + r
            q = q_ref[0, :, hd * HEAD_DIM:(hd + 1) * HEAD_DIM]
            s = lax.dot_general(q, k, (((1,), (1,)), ((), ())), preferred_element_type=F32)
            s = jnp.where(valid, s, NEG_INF)
            sk = sink_ref[hd] * LOG2E
            m = jnp.maximum(jnp.max(s, axis=-1, keepdims=True), sk)
            e = jnp.exp2(s - m)
            den = jnp.sum(e, axis=-1, keepdims=True) + jnp.exp2(sk - m)
            o = jnp.dot(e.astype(BF16), v, preferred_element_type=F32) / den
            o_ref[0, :, hd * HEAD_DIM:(hd + 1) * HEAD_DIM] = o.astype(BF16)


def _attn_a(rope3, v3, sink, *, tq=256):
    b, s, _ = rope3.shape
    nq = s // tq
    nblk = s // WINDOW
    per = tq // WINDOW
    ka_blk = A_Q // A_KV
    prev_map = lambda bi, i: (bi, jnp.maximum(i * per - 1, 0), ka_blk)
    cur_map = lambda bi, i: (bi, i, ka_blk)
    next_map = lambda bi, i: (bi, jnp.minimum((i + 1) * per, nblk - 1), ka_blk)
    vprev_map = lambda bi, i: (bi, jnp.maximum(i * per - 1, 0), 0)
    vcur_map = lambda bi, i: (bi, i, 0)
    vnext_map = lambda bi, i: (bi, jnp.minimum((i + 1) * per, nblk - 1), 0)
    return pl.pallas_call(
        functools.partial(_attn_a_body, tq=tq, nq=nq),
        grid=(b, nq),
        in_specs=[
            pl.BlockSpec(memory_space=pltpu.SMEM),
            pl.BlockSpec((1, tq, A_Q), lambda bi, i: (bi, i, 0)),
            pl.BlockSpec((1, WINDOW, A_KV), prev_map),
            pl.BlockSpec((1, tq, A_KV), cur_map),
            pl.BlockSpec((1, WINDOW, A_KV), next_map),
            pl.BlockSpec((1, WINDOW, A_KV), vprev_map),
            pl.BlockSpec((1, tq, A_KV), vcur_map),
            pl.BlockSpec((1, WINDOW, A_KV), vnext_map),
        ],
        out_specs=pl.BlockSpec((1, tq, A_Q), lambda bi, i: (bi, i, 0)),
        out_shape=jax.ShapeDtypeStruct((b, s, A_Q), BF16),
        compiler_params=pltpu.CompilerParams(
            dimension_semantics=("parallel", "parallel"), vmem_limit_bytes=VMEM_LIMIT),
        name="attn_a",
    )(sink, rope3, rope3, rope3, rope3, v3, v3, v3)


def _attn_b_body(q_ref, k_ref, v_ref, lam_ref, g_ref, o_ref,
                 s_scr, mx_scr, p_scr, a_scr, m_scr, l_scr, acc_scr, kn_scr, *, tk, nk, lam_init):
    reps = tk // LANES
    tq = q_ref.shape[1]

    def key_chunk(c):
        return k_ref[0, pl.ds(pl.multiple_of(c * tk, tk), tk), :]

    def val_chunk(c):
        return v_ref[0, pl.ds(pl.multiple_of(c * tk, tk), tk), :]

    def qk(c, comp):
        q = q_ref[0, :, comp * HEAD_DIM:(comp + 1) * HEAD_DIM]
        k = key_chunk(c)[:, comp * HEAD_DIM:(comp + 1) * HEAD_DIM]
        return lax.dot_general(q, k, (((1,), (1,)), ((), ())), preferred_element_type=F32)

    def lane_fold(x, op):
        y = x[:, :LANES]
        for j in range(1, reps):
            y = op(y, x[:, j * LANES:(j + 1) * LANES])
        return y

    @pl.when(pl.program_id(2) == 0)
    def _():
        def chunk_max(c, carry):
            kc = key_chunk(c).astype(F32)
            sq = kc * kc
            out = []
            for comp in range(2):
                n2 = jnp.sum(sq[:, comp * HEAD_DIM:(comp + 1) * HEAD_DIM], axis=-1, keepdims=True)
                out.append(jnp.maximum(carry[comp], jnp.max(n2, axis=0, keepdims=True)))
            return tuple(out)
        zero = jnp.zeros((1, 1), F32)
        n2max = lax.fori_loop(0, nk, chunk_max, (zero, zero))
        for comp in range(2):
            kn_scr[comp] = jnp.broadcast_to(jnp.sqrt(n2max[comp]), kn_scr.shape[1:])

    def scores(c, slot):
        for comp in range(2):
            s = qk(c, comp)
            s_scr[slot, comp] = s
            mx_scr[slot, comp] = lane_fold(s, jnp.maximum)

    def softmax(slot):
        for comp in range(2):
            m_prev = m_scr[comp]
            m_new = jnp.maximum(m_prev, jnp.max(mx_scr[slot, comp], axis=-1, keepdims=True))
            a_scr[slot, comp] = jnp.exp2(m_prev - m_new)
            m_scr[comp] = m_new
            e = jnp.exp2(s_scr[slot, comp] - _lane_tile(m_new, reps))
            l_scr[comp] = a_scr[slot, comp] * l_scr[comp] + lane_fold(e, jnp.add)
            p_scr[slot, comp] = e.astype(BF16)

    def pv(c, slot):
        vc = val_chunk(c)
        for comp in range(2):
            acc_scr[comp] = _lane_tile(a_scr[slot, comp], 2) * acc_scr[comp] + jnp.dot(
                p_scr[slot, comp], vc, preferred_element_type=F32)

    def steady_pair(i):
        pv(i - 1, 0)
        scores(i + 1, 0)
        softmax(1)
        pv(i, 1)
        scores(i + 2, 1)
        softmax(0)

    def steady_quad(j, carry):
        steady_pair(4 * j + 1)
        steady_pair(4 * j + 3)
        return carry

    def probs(c, slot, s_of):
        for comp in range(2):
            e = jnp.exp2(s_of(comp) - _lane_tile(m_scr[comp], reps))
            l_scr[comp] += lane_fold(e, jnp.add)
            p_scr[slot, comp] = e.astype(BF16)

    def pv_plain(c, slot):
        vc = val_chunk(c)
        for comp in range(2):
            acc_scr[comp] += jnp.dot(p_scr[slot, comp], vc, preferred_element_type=F32)

    def fixed_group(t, carry):
        for r in range(FIXED_GROUP):
            c = 2 + FIXED_GROUP * t + r
            pv_plain(c - 2, r % 2)
            probs(c, r % 2, lambda comp: qk(c, comp))
        return carry

    scores(0, 0)
    scores(1, 1)
    span = jnp.zeros((1, 1), F32)
    for comp in range(2):
        qf = q_ref[0, :, comp * HEAD_DIM:(comp + 1) * HEAD_DIM].astype(F32)
        q_norm = jnp.sqrt(jnp.sum(qf * qf, axis=-1, keepdims=True))
        bound = q_norm * kn_scr[comp][:1, :1]
        first_max = jnp.max(mx_scr[0, comp], axis=-1, keepdims=True)
        span = jnp.maximum(span, jnp.max(bound - first_max, axis=0, keepdims=True))
    use_fixed = span[0, 0] <= FIXED_SHIFT_SPAN

    l_scr[...] = jnp.zeros_like(l_scr)
    acc_scr[...] = jnp.zeros_like(acc_scr)

    @pl.when(use_fixed)
    def _():
        for comp in range(2):
            m_scr[comp] = jnp.broadcast_to(
                jnp.max(mx_scr[0, comp], axis=-1, keepdims=True), (tq, LANES))
        probs(0, 0, lambda comp: s_scr[0, comp])
        probs(1, 1, lambda comp: s_scr[1, comp])
        lax.fori_loop(0, (nk - 2) // FIXED_GROUP, fixed_group, 0)
        pv_plain(nk - 2, 0)
        pv_plain(nk - 1, 1)

    @pl.when(jnp.logical_not(use_fixed))
    def _():
        m_scr[...] = jnp.full_like(m_scr, NEG_INF)
        softmax(0)
        n_pairs = (nk - 2) // 2
        lax.fori_loop(0, n_pairs // 2, steady_quad, 0)
        if n_pairs % 2:
            steady_pair(nk - 3)
        softmax(1)
        pv(nk - 2, 0)
        pv(nk - 1, 1)

    lam_v = lam_ref[...]
    lam = (jnp.exp(jnp.sum(lam_v[0:1] * lam_v[1:2], axis=-1, keepdims=True))
           - jnp.exp(jnp.sum(lam_v[2:3] * lam_v[3:4], axis=-1, keepdims=True))
           + lam_init)
    l0 = jnp.sum(l_scr[0], axis=-1, keepdims=True)
    l1 = jnp.sum(l_scr[1], axis=-1, keepdims=True)
    o = acc_scr[0] / l0 - lam * (acc_scr[1] / l1)
    o_ref[0] = (_rms(o, g_ref[...]) * (1.0 - lam_init)).astype(BF16)


def _attn_b(rope3, v3, lam_vecs, subln_g, lam_init, *, tq=512, tk=512):
    b, s, _ = rope3.shape
    nk = s // tk
    assert nk % 2 == 0 and nk >= 4 and (nk - 2) % FIXED_GROUP == 0 and FIXED_GROUP % 2 == 0
    w = 2 * HEAD_DIM
    qb_blk = (A_Q + A_KV) // w
    kb_blk = qb_blk + B_HEADS
    vb_blk = A_KV // w
    return pl.pallas_call(
        functools.partial(_attn_b_body, tk=tk, nk=nk, lam_init=lam_init),
        grid=(b, B_HEADS, s // tq),
        in_specs=[
            pl.BlockSpec((1, tq, w), lambda bi, h, i: (bi, i, qb_blk + h)),
            pl.BlockSpec((1, s, w), lambda bi, h, i: (bi, 0, kb_blk + h)),
            pl.BlockSpec((1, s, w), lambda bi, h, i: (bi, 0, vb_blk + h)),
            pl.BlockSpec((4, HEAD_DIM), lambda bi, h, i: (0, 0)),
            pl.BlockSpec((1, w), lambda bi, h, i: (0, 0)),
        ],
        out_specs=pl.BlockSpec((1, tq, w), lambda bi, h, i: (bi, i, h)),
        out_shape=jax.ShapeDtypeStruct((b, s, B_V), BF16),
        scratch_shapes=[pltpu.VMEM((2, 2, tq, tk), F32),
                        pltpu.VMEM((2, 2, tq, LANES), F32),
                        pltpu.VMEM((2, 2, tq, tk), BF16),
                        pltpu.VMEM((2, 2, tq, LANES), F32),
                        pltpu.VMEM((2, tq, LANES), F32),
                        pltpu.VMEM((2, tq, LANES), F32),
                        pltpu.VMEM((2, tq, w), F32),
                        pltpu.VMEM((2, 8, LANES), F32)],
        compiler_params=pltpu.CompilerParams(
            dimension_semantics=("parallel", "parallel", "arbitrary"),
            vmem_limit_bytes=VMEM_LIMIT),
        name="attn_b",
    )(rope3, rope3, v3, lam_vecs, subln_g)


def _merge_body(oa_ref, ob_ref, gate_ref, x_ref, wa_ref, wb_ref, wo_ref, g_ref, o_ref):
    pa = jnp.dot(oa_ref[...], wa_ref[...], preferred_element_type=F32)
    pb = jnp.dot(ob_ref[...], wb_ref[...], preferred_element_type=F32)
    ga = gate_ref[:, :D_MODEL].astype(F32)
    gb = gate_ref[:, D_MODEL:].astype(F32)
    merged = (ga * pa + gb * pb).astype(BF16)
    y = jnp.dot(merged, wo_ref[...], preferred_element_type=F32)
    o_ref[...] = x_ref[...] + _rms(y, g_ref[...])


def _merge(oa, ob, gates, x1, wa, wb, wo, g_post, *, tm=512):
    t, d = x1.shape
    rows = lambda n: pl.BlockSpec((tm, n), lambda i: (i, 0))
    return pl.pallas_call(
        _merge_body,
        grid=(t // tm,),
        in_specs=[rows(A_Q), rows(B_V), rows(GATE_COLS), rows(d),
                  _resident(wa.shape), _resident(wb.shape), _resident(wo.shape),
                  pl.BlockSpec((1, d), lambda i: (0, 0))],
        out_specs=rows(d),
        out_shape=jax.ShapeDtypeStruct((t, d), F32),
        compiler_params=pltpu.CompilerParams(
            dimension_semantics=("parallel",), vmem_limit_bytes=VMEM_LIMIT),
        name="merge",
    )(oa, ob, gates, x1, wa, wb, wo, g_post)


def _rope_tables(seq):
    pos = jnp.arange(seq, dtype=F32)
    inv = ROPE_THETA ** (-jnp.arange(0, HEAD_DIM, 2, dtype=F32) / HEAD_DIM)
    ang = pos[:, None] * inv[None, :]
    cos, sin = jnp.cos(ang), jnp.sin(ang)
    return jnp.concatenate([cos, cos], axis=-1), jnp.concatenate([-sin, sin], axis=-1)


def kernel(x, ffn1_pre_g, ffn1_w_gate, ffn1_w_up, ffn1_w_down, ffn1_post_g, mix_pre_g, w_in, gate_bias, sink_logit, lambda_q1, lambda_k1, lambda_q2, lambda_k2, subln_g, w_proj_a, w_proj_b, w_out, mix_post_g, ffn2_pre_g, ffn2_w_gate, ffn2_w_up, ffn2_w_down, ffn2_post_g):
    b, s, d = x.shape
    t = b * s
    depth = ffn1_pre_g.shape[0]
    cos2, sin2 = _rope_tables(s)
    xt = x.reshape(t, d)
    q_scaled = (QK_SCALE,) * A_Q_HEADS + (1.0,) * A_KV_HEADS + (QK_SCALE,) * (2 * B_HEADS) + (1.0,) * (2 * B_HEADS)
    tm_proj = 512
    for l in range(depth):
        lam_init = 0.8 - 0.6 * math.exp(-0.3 * l)
        row = lambda p: p[l:l + 1]
        bf = lambda p: p[l].astype(BF16)

        xt, h = _ffn(xt, row(ffn1_pre_g), bf(ffn1_w_gate), bf(ffn1_w_up), bf(ffn1_w_down),
                     row(ffn1_post_g), row(mix_pre_g))

        w = w_in[l]
        qa_end, ka_end, va_end = A_Q, A_Q + A_KV, A_Q + 2 * A_KV
        qb_end = va_end + B_QK
        kb_end = qb_end + B_QK
        vb_end = kb_end + B_V
        w_rope = jnp.concatenate([w[:, :ka_end], w[:, va_end:kb_end]], axis=1).astype(BF16)
        w_v = jnp.concatenate([w[:, ka_end:va_end], w[:, kb_end:vb_end]], axis=1).astype(BF16)
        w_gate = w[:, vb_end:].astype(BF16)

        ns = s // tm_proj
        tab = pl.BlockSpec((tm_proj, HEAD_DIM), lambda i: (i % ns, 0))
        rope = _proj_call(functools.partial(_proj_rope_body, head_scales=q_scaled),
                          h, w_rope, [cos2, sin2], [tab, tab], ROPE_COLS, "proj_rope", tm=tm_proj)
        vals = _proj_call(functools.partial(_proj_plain_body, chunk=MXU_N),
                          h, w_v, [], [], V_COLS, "proj_v", tm=tm_proj)
        gates = _proj_call(functools.partial(_proj_gate_body, chunk=2 * MXU_N),
                           h, w_gate, [gate_bias[l:l + 1]], [_resident((1, GATE_COLS))],
                           GATE_COLS, "proj_gate", tm=tm_proj)

        rope3 = rope.reshape(b, s, ROPE_COLS)
        v3 = vals.reshape(b, s, V_COLS)
        out_a = _attn_a(rope3, v3, sink_logit[l])
        lam_vecs = jnp.stack([lambda_q1[l], lambda_k1[l], lambda_q2[l], lambda_k2[l]]).astype(F32)
        out_b = _attn_b(rope3, v3, lam_vecs, row(subln_g), lam_init)

        xt = _merge(out_a.reshape(t, A_Q), out_b.reshape(t, B_V), gates, xt,
                    bf(w_proj_a), bf(w_proj_b), bf(w_out), row(mix_post_g))

        xt = _ffn(xt, row(ffn2_pre_g), bf(ffn2_w_gate), bf(ffn2_w_up), bf(ffn2_w_down),
                  row(ffn2_post_g))
    return xt.reshape(b, s, d)
```

```python
import functools
import math

import jax
import jax.numpy as jnp
from jax import lax
from jax.experimental import pallas as pl
from jax.experimental.pallas import tpu as pltpu

D_MODEL = 2048
HEAD_DIM = 128
A_Q_HEADS = 8
A_KV_HEADS = 2
A_REP = A_Q_HEADS // A_KV_HEADS
WINDOW = 128
B_HEADS = 4
D_FF = 5632
ROPE_THETA = 10000.0
EPS = 1e-6
NEG_INF = -1e30
LOG2E = 1.4426950408889634
QK_SCALE = HEAD_DIM ** -0.5 * LOG2E
FIXED_SHIFT_SPAN = 60.0
FIXED_GROUP = 10

A_Q = A_Q_HEADS * HEAD_DIM
A_KV = A_KV_HEADS * HEAD_DIM
B_QK = B_HEADS * 2 * HEAD_DIM
B_V = B_HEADS * 2 * HEAD_DIM
ROPE_COLS = A_Q + A_KV + 2 * B_QK
V_COLS = A_KV + B_V
GATE_COLS = 2 * D_MODEL

LANES = 128
MXU_N = 256
VMEM_LIMIT = 56 * 1024 * 1024

F32 = jnp.float32
BF16 = jnp.bfloat16


def _rms(x, g):
    ms = jnp.mean(x * x, axis=-1, keepdims=True)
    return x * lax.rsqrt(ms + EPS) * g


def _lane_tile(x, n):
    return jnp.concatenate([x] * n, axis=1)


def _resident(shape):
    return pl.BlockSpec(shape, lambda *_: (0,) * len(shape), pipeline_mode=pl.Buffered(1))


def _ffn_body(*refs, nf, tf, n_sub_last, emit_next):
    if emit_next:
        (x_ref, gpre_ref, wg_ref, wu_ref, wd_ref, gpost_ref, gnext_ref,
         o_ref, hn_ref, h_scr) = refs
    else:
        (x_ref, gpre_ref, wg_ref, wu_ref, wd_ref, gpost_ref, o_ref, h_scr) = refs
    f = pl.program_id(1)
    n_sub = wg_ref.shape[1] // tf

    def sub_blocks(n, h, first=False):
        for j in range(n):
            cols = slice(j * tf, (j + 1) * tf)
            gate = jnp.dot(h, wg_ref[:, cols], preferred_element_type=F32)
            up = jnp.dot(h, wu_ref[:, cols], preferred_element_type=F32)
            a = gate * (1.0 / (1.0 + jnp.exp(-gate))) * up
            y = jnp.dot(a.astype(BF16), wd_ref[cols, :], preferred_element_type=F32)
            if first and j == 0:
                o_ref[...] = y
            else:
                o_ref[...] += y

    @pl.when(f == 0)
    def _():
        h = _rms(x_ref[...], gpre_ref[...]).astype(BF16)
        h_scr[...] = h
        sub_blocks(n_sub, h, first=True)

    @pl.when((f > 0) & (f < nf - 1))
    def _():
        sub_blocks(n_sub, h_scr[...])

    @pl.when(f == nf - 1)
    def _():
        sub_blocks(n_sub_last, h_scr[...])
        xo = x_ref[...] + 0.5 * _rms(o_ref[...], gpost_ref[...])
        o_ref[...] = xo
        if emit_next:
            hn_ref[...] = _rms(xo, gnext_ref[...]).astype(BF16)


def _ffn(x, g_pre, wg, wu, wd, g_post, g_next=None, *, tm=512, tf=512, n_sub=2):
    t, d = x.shape
    blk = tf * n_sub
    nf = pl.cdiv(D_FF, blk)
    n_sub_last = (D_FF - (nf - 1) * blk) // tf
    assert (nf - 1) * blk + n_sub_last * tf == D_FF
    emit_next = g_next is not None
    row = pl.BlockSpec((tm, d), lambda i, f: (i, 0))
    vec = pl.BlockSpec((1, d), lambda i, f: (0, 0))
    in_specs = [row, vec,
                pl.BlockSpec((d, blk), lambda i, f: (0, f)),
                pl.BlockSpec((d, blk), lambda i, f: (0, f)),
                pl.BlockSpec((blk, d), lambda i, f: (f, 0)),
                vec]
    args = [x, g_pre, wg, wu, wd, g_post]
    out_shape = [jax.ShapeDtypeStruct((t, d), F32)]
    out_specs = [row]
    if emit_next:
        in_specs.append(vec)
        args.append(g_next)
        out_shape.append(jax.ShapeDtypeStruct((t, d), BF16))
        out_specs.append(row)
    res = pl.pallas_call(
        functools.partial(_ffn_body, nf=nf, tf=tf, n_sub_last=n_sub_last, emit_next=emit_next),
        grid=(t // tm, nf),
        in_specs=in_specs,
        out_specs=out_specs,
        out_shape=out_shape,
        scratch_shapes=[pltpu.VMEM((tm, d), BF16)],
        compiler_params=pltpu.CompilerParams(
            dimension_semantics=("parallel", "arbitrary"), vmem_limit_bytes=VMEM_LIMIT),
        name="ffn_next" if emit_next else "ffn",
    )(*args)
    return res if emit_next else res[0]


def _proj_rope_body(h_ref, w_ref, cos_ref, sin_ref, o_ref, *, head_scales):
    h = h_ref[...]
    cos = cos_ref[...]
    sin = sin_ref[...]
    heads_per_chunk = MXU_N // HEAD_DIM
    for c in range(len(head_scales) // heads_per_chunk):
        acc = jnp.dot(h, w_ref[:, c * MXU_N:(c + 1) * MXU_N], preferred_element_type=F32)
        for k in range(heads_per_chunk):
            hd = c * heads_per_chunk + k
            a = acc[:, k * HEAD_DIM:(k + 1) * HEAD_DIM]
            r = a * cos + pltpu.roll(a, HEAD_DIM // 2, 1) * sin
            if head_scales[hd] != 1.0:
                r = r * head_scales[hd]
            o_ref[:, hd * HEAD_DIM:(hd + 1) * HEAD_DIM] = r.astype(BF16)


def _proj_plain_body(h_ref, w_ref, o_ref, *, chunk):
    h = h_ref[...]
    for c in range(w_ref.shape[1] // chunk):
        acc = jnp.dot(h, w_ref[:, c * chunk:(c + 1) * chunk], preferred_element_type=F32)
        o_ref[:, c * chunk:(c + 1) * chunk] = acc.astype(BF16)


def _proj_gate_body(h_ref, w_ref, b_ref, o_ref, *, chunk):
    h = h_ref[...]
    for c in range(w_ref.shape[1] // chunk):
        acc = jnp.dot(h, w_ref[:, c * chunk:(c + 1) * chunk], preferred_element_type=F32)
        z = acc + b_ref[:, c * chunk:(c + 1) * chunk]
        o_ref[:, c * chunk:(c + 1) * chunk] = (1.0 / (1.0 + jnp.exp(-z))).astype(BF16)


def _proj_call(body, h, w, extra, extra_specs, n_out, name, *, tm):
    t, d = h.shape
    return pl.pallas_call(
        body,
        grid=(t // tm,),
        in_specs=[pl.BlockSpec((tm, d), lambda i: (i, 0)), _resident(w.shape)] + extra_specs,
        out_specs=pl.BlockSpec((tm, n_out), lambda i: (i, 0)),
        out_shape=jax.ShapeDtypeStruct((t, n_out), BF16),
        compiler_params=pltpu.CompilerParams(
            dimension_semantics=("parallel",), vmem_limit_bytes=VMEM_LIMIT),
        name=name,
    )(h, w, *extra)


def _attn_a_body(sink_ref, q_ref, kp_ref, kc_ref, kn_ref, vp_ref, vc_ref, vn_ref, o_ref, *, tq, nq):
    i = pl.program_id(1)
    nk = tq + 2 * WINDOW
    row = lax.broadcasted_iota(jnp.int32, (tq, nk), 0)
    col = lax.broadcasted_iota(jnp.int32, (tq, nk), 1)
    rel = col - WINDOW - row
    valid = (rel >= -WINDOW) & (rel <= WINDOW)
    valid = valid & ((col >= WINDOW) | (i > 0))
    valid = valid & ((col < tq + WINDOW) | (i < nq - 1))
    k_all = jnp.concatenate([kp_ref[0], kc_ref[0], kn_ref[0]], axis=0)
    v_all = jnp.concatenate([vp_ref[0], vc_ref[0], vn_ref[0]], axis=0)
    for g in range(A_KV_HEADS):
        k = k_all[:, g * HEAD_DIM:(g + 1) * HEAD_DIM]
        v = v_all[:, g * HEAD_DIM:(g + 1) * HEAD_DIM]
        for r in range(A_REP):
            hd = g * A_REP + r
            q = q_ref[0, :, hd * HEAD_DIM:(hd + 1) * HEAD_DIM]
            s = lax.dot_general(q, k, (((1,), (1,)), ((), ())), preferred_element_type=F32)
            s = jnp.where(valid, s, NEG_INF)
            sk = sink_ref[hd] * LOG2E
            m = jnp.maximum(jnp.max(s, axis=-1, keepdims=True), sk)
            e = jnp.exp2(s - m)
            den = jnp.sum(e, axis=-1, keepdims=True) + jnp.exp2(sk - m)
            o = jnp.dot(e.astype(BF16), v, preferred_element_type=F32) / den
            o_ref[0, :, hd * HEAD_DIM:(hd + 1) * HEAD_DIM] = o.astype(BF16)


def _attn_a(rope3, v3, sink, *, tq=256):
    b, s, _ = rope3.shape
    nq = s // tq
    nblk = s // WINDOW
    per = tq // WINDOW
    ka_blk = A_Q // A_KV
    prev_map = lambda bi, i: (bi, jnp.maximum(i * per - 1, 0), ka_blk)
    cur_map = lambda bi, i: (bi, i, ka_blk)
    next_map = lambda bi, i: (bi, jnp.minimum((i + 1) * per, nblk - 1), ka_blk)
    vprev_map = lambda bi, i: (bi, jnp.maximum(i * per - 1, 0), 0)
    vcur_map = lambda bi, i: (bi, i, 0)
    vnext_map = lambda bi, i: (bi, jnp.minimum((i + 1) * per, nblk - 1), 0)
    return pl.pallas_call(
        functools.partial(_attn_a_body, tq=tq, nq=nq),
        grid=(b, nq),
        in_specs=[
            pl.BlockSpec(memory_space=pltpu.SMEM),
            pl.BlockSpec((1, tq, A_Q), lambda bi, i: (bi, i, 0)),
            pl.BlockSpec((1, WINDOW, A_KV), prev_map),
            pl.BlockSpec((1, tq, A_KV), cur_map),
            pl.BlockSpec((1, WINDOW, A_KV), next_map),
            pl.BlockSpec((1, WINDOW, A_KV), vprev_map),
            pl.BlockSpec((1, tq, A_KV), vcur_map),
            pl.BlockSpec((1, WINDOW, A_KV), vnext_map),
        ],
        out_specs=pl.BlockSpec((1, tq, A_Q), lambda bi, i: (bi, i, 0)),
        out_shape=jax.ShapeDtypeStruct((b, s, A_Q), BF16),
        compiler_params=pltpu.CompilerParams(
            dimension_semantics=("parallel", "parallel"), vmem_limit_bytes=VMEM_LIMIT),
        name="attn_a",
    )(sink, rope3, rope3, rope3, rope3, v3, v3, v3)


def _attn_b_body(q_ref, k_ref, v_ref, lam_ref, g_ref, o_ref,
                 s_scr, mx_scr, p_scr, a_scr, m_scr, l_scr, acc_scr, kn_scr, *, tk, nk, lam_init):
    reps = tk // LANES
    tq = q_ref.shape[1]

    def key_chunk(c):
        return k_ref[0, pl.ds(pl.multiple_of(c * tk, tk), tk), :]

    def val_chunk(c):
        return v_ref[0, pl.ds(pl.multiple_of(c * tk, tk), tk), :]

    def qk(c, comp):
        q = q_ref[0, :, comp * HEAD_DIM:(comp + 1) * HEAD_DIM]
        k = key_chunk(c)[:, comp * HEAD_DIM:(comp + 1) * HEAD_DIM]
        return lax.dot_general(q, k, (((1,), (1,)), ((), ())), preferred_element_type=F32)

    def lane_fold(x, op):
        y = x[:, :LANES]
        for j in range(1, reps):
            y = op(y, x[:, j * LANES:(j + 1) * LANES])
        return y

    @pl.when(pl.program_id(2) == 0)
    def _():
        def chunk_max(c, carry):
            kc = key_chunk(c).astype(F32)
            sq = kc * kc
            out = []
            for comp in range(2):
                n2 = jnp.sum(sq[:, comp * HEAD_DIM:(comp + 1) * HEAD_DIM], axis=-1, keepdims=True)
                out.append(jnp.maximum(carry[comp], jnp.max(n2, axis=0, keepdims=True)))
            return tuple(out)
        zero = jnp.zeros((1, 1), F32)
        n2max = lax.fori_loop(0, nk, chunk_max, (zero, zero))
        for comp in range(2):
            kn_scr[comp] = jnp.broadcast_to(jnp.sqrt(n2max[comp]), kn_scr.shape[1:])

    def scores(c, slot):
        for comp in range(2):
            s = qk(c, comp)
            s_scr[slot, comp] = s
            mx_scr[slot, comp] = lane_fold(s, jnp.maximum)

    def softmax(slot):
        for comp in range(2):
            m_prev = m_scr[comp]
            m_new = jnp.maximum(m_prev, jnp.max(mx_scr[slot, comp], axis=-1, keepdims=True))
            a_scr[slot, comp] = jnp.exp2(m_prev - m_new)
            m_scr[comp] = m_new
            e = jnp.exp2(s_scr[slot, comp] - _lane_tile(m_new, reps))
            l_scr[comp] = a_scr[slot, comp] * l_scr[comp] + lane_fold(e, jnp.add)
            p_scr[slot, comp] = e.astype(BF16)

    def pv(c, slot):
        vc = val_chunk(c)
        for comp in range(2):
            acc_scr[comp] = _lane_tile(a_scr[slot, comp], 2) * acc_scr[comp] + jnp.dot(
                p_scr[slot, comp], vc, preferred_element_type=F32)

    def steady_pair(i):
        pv(i - 1, 0)
        scores(i + 1, 0)
        softmax(1)
        pv(i, 1)
        scores(i + 2, 1)
        softmax(0)

    def steady_quad(j, carry):
        steady_pair(4 * j + 1)
        steady_pair(4 * j + 3)
        return carry

    def probs(c, slot, s_of):
        for comp in range(2):
            e = jnp.exp2(s_of(comp) - _lane_tile(m_scr[comp], reps))
            l_scr[comp] += lane_fold(e, jnp.add)
            p_scr[slot, comp] = e.astype(BF16)

    def pv_plain(c, slot):
        vc = val_chunk(c)
        for comp in range(2):
            acc_scr[comp] += jnp.dot(p_scr[slot, comp], vc, preferred_element_type=F32)

    def fixed_group(t, carry):
        for r in range(FIXED_GROUP):
            c = 2 + FIXED_GROUP * t + r
            pv_plain(c - 2, r % 2)
            probs(c, r % 2, lambda comp: qk(c, comp))
        return carry

    l_scr[...] = jnp.zeros_like(l_scr)
    acc_scr[...] = jnp.zeros_like(acc_scr)
    scores(0, 0)
    scores(1, 1)
    span = jnp.zeros((1, 1), F32)
    for comp in range(2):
        qf = q_ref[0, :, comp * HEAD_DIM:(comp + 1) * HEAD_DIM].astype(F32)
        q_norm = jnp.sqrt(jnp.sum(qf * qf, axis=-1, keepdims=True))
        bound = q_norm * kn_scr[comp][:1, :1]
        first_max = jnp.max(mx_scr[0, comp], axis=-1, keepdims=True)
        span = jnp.maximum(span, jnp.max(bound - first_max, axis=0, keepdims=True))
        m_scr[comp] = jnp.broadcast_to(first_max, (tq, LANES))
    use_fixed = span[0, 0] <= FIXED_SHIFT_SPAN
    probs(0, 0, lambda comp: s_scr[0, comp])

    @pl.when(use_fixed)
    def _():
        probs(1, 1, lambda comp: s_scr[1, comp])
        lax.fori_loop(0, (nk - 2) // FIXED_GROUP, fixed_group, 0)
        pv_plain(nk - 2, 0)
        pv_plain(nk - 1, 1)

    @pl.when(jnp.logical_not(use_fixed))
    def _():
        m_scr[...] = jnp.full_like(m_scr, NEG_INF)
        l_scr[...] = jnp.zeros_like(l_scr)
        softmax(0)
        n_pairs = (nk - 2) // 2
        lax.fori_loop(0, n_pairs // 2, steady_quad, 0)
        if n_pairs % 2:
            steady_pair(nk - 3)
        softmax(1)
        pv(nk - 2, 0)
        pv(nk - 1, 1)

    lam_v = lam_ref[...]
    lam = (jnp.exp(jnp.sum(lam_v[0:1] * lam_v[1:2], axis=-1, keepdims=True))
           - jnp.exp(jnp.sum(lam_v[2:3] * lam_v[3:4], axis=-1, keepdims=True))
           + lam_init)
    l0 = jnp.sum(l_scr[0], axis=-1, keepdims=True)
    l1 = jnp.sum(l_scr[1], axis=-1, keepdims=True)
    o = acc_scr[0] / l0 - lam * (acc_scr[1] / l1)
    o_ref[0] = (_rms(o, g_ref[...]) * (1.0 - lam_init)).astype(BF16)


def _attn_b(rope3, v3, lam_vecs, subln_g, lam_init, *, tq=512, tk=512):
    b, s, _ = rope3.shape
    nk = s // tk
    assert nk % 2 == 0 and nk >= 4 and (nk - 2) % FIXED_GROUP == 0 and FIXED_GROUP % 2 == 0
    w = 2 * HEAD_DIM
    qb_blk = (A_Q + A_KV) // w
    kb_blk = qb_blk + B_HEADS
    vb_blk = A_KV // w
    return pl.pallas_call(
        functools.partial(_attn_b_body, tk=tk, nk=nk, lam_init=lam_init),
        grid=(b, B_HEADS, s // tq),
        in_specs=[
            pl.BlockSpec((1, tq, w), lambda bi, h, i: (bi, i, qb_blk + h)),
            pl.BlockSpec((1, s, w), lambda bi, h, i: (bi, 0, kb_blk + h)),
            pl.BlockSpec((1, s, w), lambda bi, h, i: (bi, 0, vb_blk + h)),
            pl.BlockSpec((4, HEAD_DIM), lambda bi, h, i: (0, 0)),
            pl.BlockSpec((1, w), lambda bi, h, i: (0, 0)),
        ],
        out_specs=pl.BlockSpec((1, tq, w), lambda bi, h, i: (bi, i, h)),
        out_shape=jax.ShapeDtypeStruct((b, s, B_V), BF16),
        scratch_shapes=[pltpu.VMEM((2, 2, tq, tk), F32),
                        pltpu.VMEM((2, 2, tq, LANES), F32),
                        pltpu.VMEM((2, 2, tq, tk), BF16),
                        pltpu.VMEM((2, 2, tq, LANES), F32),
                        pltpu.VMEM((2, tq, LANES), F32),
                        pltpu.VMEM((2, tq, LANES), F32),
                        pltpu.VMEM((2, tq, w), F32),
                        pltpu.VMEM((2, 8, LANES), F32)],
        compiler_params=pltpu.CompilerParams(
            dimension_semantics=("parallel", "parallel", "arbitrary"),
            vmem_limit_bytes=VMEM_LIMIT),
        name="attn_b",
    )(rope3, rope3, v3, lam_vecs, subln_g)


def _merge_body(oa_ref, ob_ref, gate_ref, x_ref, wa_ref, wb_ref, wo_ref, g_ref, o_ref):
    pa = jnp.dot(oa_ref[...], wa_ref[...], preferred_element_type=F32)
    pb = jnp.dot(ob_ref[...], wb_ref[...], preferred_element_type=F32)
    ga = gate_ref[:, :D_MODEL].astype(F32)
    gb = gate_ref[:, D_MODEL:].astype(F32)
    merged = (ga * pa + gb * pb).astype(BF16)
    y = jnp.dot(merged, wo_ref[...], preferred_element_type=F32)
    o_ref[...] = x_ref[...] + _rms(y, g_ref[...])


def _merge(oa, ob, gates, x1, wa, wb, wo, g_post, *, tm=512):
    t, d = x1.shape
    rows = lambda n: pl.BlockSpec((tm, n), lambda i: (i, 0))
    return pl.pallas_call(
        _merge_body,
        grid=(t // tm,),
        in_specs=[rows(A_Q), rows(B_V), rows(GATE_COLS), rows(d),
                  _resident(wa.shape), _resident(wb.shape), _resident(wo.shape),
                  pl.BlockSpec((1, d), lambda i: (0, 0))],
        out_specs=rows(d),
        out_shape=jax.ShapeDtypeStruct((t, d), F32),
        compiler_params=pltpu.CompilerParams(
            dimension_semantics=("parallel",), vmem_limit_bytes=VMEM_LIMIT),
        name="merge",
    )(oa, ob, gates, x1, wa, wb, wo, g_post)


def _rope_tables(seq):
    pos = jnp.arange(seq, dtype=F32)
    inv = ROPE_THETA ** (-jnp.arange(0, HEAD_DIM, 2, dtype=F32) / HEAD_DIM)
    ang = pos[:, None] * inv[None, :]
    cos, sin = jnp.cos(ang), jnp.sin(ang)
    return jnp.concatenate([cos, cos], axis=-1), jnp.concatenate([-sin, sin], axis=-1)


def kernel(x, ffn1_pre_g, ffn1_w_gate, ffn1_w_up, ffn1_w_down, ffn1_post_g, mix_pre_g, w_in, gate_bias, sink_logit, lambda_q1, lambda_k1, lambda_q2, lambda_k2, subln_g, w_proj_a, w_proj_b, w_out, mix_post_g, ffn2_pre_g, ffn2_w_gate, ffn2_w_up, ffn2_w_down, ffn2_post_g):
    b, s, d = x.shape
    t = b * s
    depth = ffn1_pre_g.shape[0]
    cos2, sin2 = _rope_tables(s)
    xt = x.reshape(t, d)
    q_scaled = (QK_SCALE,) * A_Q_HEADS + (1.0,) * A_KV_HEADS + (QK_SCALE,) * (2 * B_HEADS) + (1.0,) * (2 * B_HEADS)
    tm_proj = 512
    for l in range(depth):
        lam_init = 0.8 - 0.6 * math.exp(-0.3 * l)
        row = lambda p: p[l:l + 1]
        bf = lambda p: p[l].astype(BF16)

        xt, h = _ffn(xt, row(ffn1_pre_g), bf(ffn1_w_gate), bf(ffn1_w_up), bf(ffn1_w_down),
                     row(ffn1_post_g), row(mix_pre_g))

        w = w_in[l]
        qa_end, ka_end, va_end = A_Q, A_Q + A_KV, A_Q + 2 * A_KV
        qb_end = va_end + B_QK
        kb_end = qb_end + B_QK
        vb_end = kb_end + B_V
        w_rope = jnp.concatenate([w[:, :ka_end], w[:, va_end:kb_end]], axis=1).astype(BF16)
        w_v = jnp.concatenate([w[:, ka_end:va_end], w[:, kb_end:vb_end]], axis=1).astype(BF16)
        w_gate = w[:, vb_end:].astype(BF16)

        ns = s // tm_proj
        tab = pl.BlockSpec((tm_proj, HEAD_DIM), lambda i: (i % ns, 0))
        rope = _proj_call(functools.partial(_proj_rope_body, head_scales=q_scaled),
                          h, w_rope, [cos2, sin2], [tab, tab], ROPE_COLS, "proj_rope", tm=tm_proj)
        vals = _proj_call(functools.partial(_proj_plain_body, chunk=MXU_N),
                          h, w_v, [], [], V_COLS, "proj_v", tm=tm_proj)
        gates = _proj_call(functools.partial(_proj_gate_body, chunk=2 * MXU_N),
                           h, w_gate, [gate_bias[l:l + 1]], [_resident((1, GATE_COLS))],
                           GATE_COLS, "proj_gate", tm=tm_proj)

        rope3 = rope.reshape(b, s, ROPE_COLS)
        v3 = vals.reshape(b, s, V_COLS)
        out_a = _attn_a(rope3, v3, sink_logit[l])
        lam_vecs = jnp.stack([lambda_q1[l], lambda_k1[l], lambda_q2[l], lambda_k2[l]]).astype(F32)
        out_b = _attn_b(rope3, v3, lam_vecs, row(subln_g), lam_init)

        xt = _merge(out_a.reshape(t, A_Q), out_b.reshape(t, B_V), gates, xt,
                    bf(w_proj_a), bf(w_proj_b), bf(w_out), row(mix_post_g))

        xt = _ffn(xt, row(ffn2_pre_g), bf(ffn2_w_gate), bf(ffn2_w_up), bf(ffn2_w_down),
                  row(ffn2_post_g))
    return xt.reshape(b, s, d)
```

```python
import functools
import math

import jax
import jax.numpy as jnp
from jax import lax
from jax.experimental import pallas as pl
from jax.experimental.pallas import tpu as pltpu

D_MODEL = 2048
HEAD_DIM = 128
A_Q_HEADS = 8
A_KV_HEADS = 2
A_REP = A_Q_HEADS // A_KV_HEADS
WINDOW = 128
B_HEADS = 4
D_FF = 5632
ROPE_THETA = 10000.0
EPS = 1e-6
NEG_INF = -1e30
LOG2E = 1.4426950408889634
QK_SCALE = HEAD_DIM ** -0.5 * LOG2E
FIXED_SHIFT_SPAN = 60.0
FIXED_GROUP = 30

A_Q = A_Q_HEADS * HEAD_DIM
A_KV = A_KV_HEADS * HEAD_DIM
B_QK = B_HEADS * 2 * HEAD_DIM
B_V = B_HEADS * 2 * HEAD_DIM
ROPE_COLS = A_Q + A_KV + 2 * B_QK
V_COLS = A_KV + B_V
GATE_COLS = 2 * D_MODEL

LANES = 128
MXU_N = 256
VMEM_LIMIT = 56 * 1024 * 1024

F32 = jnp.float32
BF16 = jnp.bfloat16


def _rms(x, g):
    ms = jnp.mean(x * x, axis=-1, keepdims=True)
    return x * lax.rsqrt(ms + EPS) * g


def _lane_tile(x, n):
    return jnp.concatenate([x] * n, axis=1)


def _resident(shape):
    return pl.BlockSpec(shape, lambda *_: (0,) * len(shape), pipeline_mode=pl.Buffered(1))


def _ffn_body(*refs, nf, tf, n_sub_last, emit_next):
    if emit_next:
        (x_ref, gpre_ref, wg_ref, wu_ref, wd_ref, gpost_ref, gnext_ref,
         o_ref, hn_ref, h_scr) = refs
    else:
        (x_ref, gpre_ref, wg_ref, wu_ref, wd_ref, gpost_ref, o_ref, h_scr) = refs
    f = pl.program_id(1)
    n_sub = wg_ref.shape[1] // tf

    def sub_blocks(n, h, first=False):
        for j in range(n):
            cols = slice(j * tf, (j + 1) * tf)
            gate = jnp.dot(h, wg_ref[:, cols], preferred_element_type=F32)
            up = jnp.dot(h, wu_ref[:, cols], preferred_element_type=F32)
            a = gate * (1.0 / (1.0 + jnp.exp(-gate))) * up
            y = jnp.dot(a.astype(BF16), wd_ref[cols, :], preferred_element_type=F32)
            if first and j == 0:
                o_ref[...] = y
            else:
                o_ref[...] += y

    @pl.when(f == 0)
    def _():
        h = _rms(x_ref[...], gpre_ref[...]).astype(BF16)
        h_scr[...] = h
        sub_blocks(n_sub, h, first=True)

    @pl.when((f > 0) & (f < nf - 1))
    def _():
        sub_blocks(n_sub, h_scr[...])

    @pl.when(f == nf - 1)
    def _():
        sub_blocks(n_sub_last, h_scr[...])
        xo = x_ref[...] + 0.5 * _rms(o_ref[...], gpost_ref[...])
        o_ref[...] = xo
        if emit_next:
            hn_ref[...] = _rms(xo, gnext_ref[...]).astype(BF16)


def _ffn(x, g_pre, wg, wu, wd, g_post, g_next=None, *, tm=512, tf=512, n_sub=2):
    t, d = x.shape
    blk = tf * n_sub
    nf = pl.cdiv(D_FF, blk)
    n_sub_last = (D_FF - (nf - 1) * blk) // tf
    assert (nf - 1) * blk + n_sub_last * tf == D_FF
    emit_next = g_next is not None
    row = pl.BlockSpec((tm, d), lambda i, f: (i, 0))
    vec = pl.BlockSpec((1, d), lambda i, f: (0, 0))
    in_specs = [row, vec,
                pl.BlockSpec((d, blk), lambda i, f: (0, f)),
                pl.BlockSpec((d, blk), lambda i, f: (0, f)),
                pl.BlockSpec((blk, d), lambda i, f: (f, 0)),
                vec]
    args = [x, g_pre, wg, wu, wd, g_post]
    out_shape = [jax.ShapeDtypeStruct((t, d), F32)]
    out_specs = [row]
    if emit_next:
        in_specs.append(vec)
        args.append(g_next)
        out_shape.append(jax.ShapeDtypeStruct((t, d), BF16))
        out_specs.append(row)
    res = pl.pallas_call(
        functools.partial(_ffn_body, nf=nf, tf=tf, n_sub_last=n_sub_last, emit_next=emit_next),
        grid=(t // tm, nf),
        in_specs=in_specs,
        out_specs=out_specs,
        out_shape=out_shape,
        scratch_shapes=[pltpu.VMEM((tm, d), BF16)],
        compiler_params=pltpu.CompilerParams(
            dimension_semantics=("parallel", "arbitrary"), vmem_limit_bytes=VMEM_LIMIT),
        name="ffn_next" if emit_next else "ffn",
    )(*args)
    return res if emit_next else res[0]


def _proj_rope_body(h_ref, w_ref, cos_ref, sin_ref, o_ref, *, head_scales):
    h = h_ref[...]
    cos = cos_ref[...]
    sin = sin_ref[...]
    heads_per_chunk = MXU_N // HEAD_DIM
    for c in range(len(head_scales) // heads_per_chunk):
        acc = jnp.dot(h, w_ref[:, c * MXU_N:(c + 1) * MXU_N], preferred_element_type=F32)
        for k in range(heads_per_chunk):
            hd = c * heads_per_chunk + k
            a = acc[:, k * HEAD_DIM:(k + 1) * HEAD_DIM]
            r = a * cos + pltpu.roll(a, HEAD_DIM // 2, 1) * sin
            if head_scales[hd] != 1.0:
                r = r * head_scales[hd]
            o_ref[:, hd * HEAD_DIM:(hd + 1) * HEAD_DIM] = r.astype(BF16)


def _proj_plain_body(h_ref, w_ref, o_ref, *, chunk):
    h = h_ref[...]
    for c in range(w_ref.shape[1] // chunk):
        acc = jnp.dot(h, w_ref[:, c * chunk:(c + 1) * chunk], preferred_element_type=F32)
        o_ref[:, c * chunk:(c + 1) * chunk] = acc.astype(BF16)


def _proj_gate_body(h_ref, w_ref, b_ref, o_ref, *, chunk):
    h = h_ref[...]
    for c in range(w_ref.shape[1] // chunk):
        acc = jnp.dot(h, w_ref[:, c * chunk:(c + 1) * chunk], preferred_element_type=F32)
        z = acc + b_ref[:, c * chunk:(c + 1) * chunk]
        o_ref[:, c * chunk:(c + 1) * chunk] = (1.0 / (1.0 + jnp.exp(-z))).astype(BF16)


def _proj_call(body, h, w, extra, extra_specs, n_out, name, *, tm):
    t, d = h.shape
    return pl.pallas_call(
        body,
        grid=(t // tm,),
        in_specs=[pl.BlockSpec((tm, d), lambda i: (i, 0)), _resident(w.shape)] + extra_specs,
        out_specs=pl.BlockSpec((tm, n_out), lambda i: (i, 0)),
        out_shape=jax.ShapeDtypeStruct((t, n_out), BF16),
        compiler_params=pltpu.CompilerParams(
            dimension_semantics=("parallel",), vmem_limit_bytes=VMEM_LIMIT),
        name=name,
    )(h, w, *extra)


def _attn_a_body(sink_ref, q_ref, kp_ref, kc_ref, kn_ref, vp_ref, vc_ref, vn_ref, o_ref, *, tq, nq):
    i = pl.program_id(1)
    nk = tq + 2 * WINDOW
    row = lax.broadcasted_iota(jnp.int32, (tq, nk), 0)
    col = lax.broadcasted_iota(jnp.int32, (tq, nk), 1)
    rel = col - WINDOW - row
    valid = (rel >= -WINDOW) & (rel <= WINDOW)
    valid = valid & ((col >= WINDOW) | (i > 0))
    valid = valid & ((col < tq + WINDOW) | (i < nq - 1))
    k_all = jnp.concatenate([kp_ref[0], kc_ref[0], kn_ref[0]], axis=0)
    v_all = jnp.concatenate([vp_ref[0], vc_ref[0], vn_ref[0]], axis=0)
    for g in range(A_KV_HEADS):
        k = k_all[:, g * HEAD_DIM:(g + 1) * HEAD_DIM]
        v = v_all[:, g * HEAD_DIM:(g + 1) * HEAD_DIM]
        for r in range(A_REP):
            hd = g * A_REP + r
            q = q_ref[0, :, hd * HEAD_DIM:(hd + 1) * HEAD_DIM]
            s = lax.dot_general(q, k, (((1,), (1,)), ((), ())), preferred_element_type=F32)
            s = jnp.where(valid, s, NEG_INF)
            sk = sink_ref[hd] * LOG2E
            m = jnp.maximum(jnp.max(s, axis=-1, keepdims=True), sk)
            e = jnp.exp2(s - m)
            den = jnp.sum(e, axis=-1, keepdims=True) + jnp.exp2(sk - m)
            o = jnp.dot(e.astype(BF16), v, preferred_element_type=F32) / den
            o_ref[0, :, hd * HEAD_DIM:(hd + 1) * HEAD_DIM] = o.astype(BF16)


def _attn_a(rope3, v3, sink, *, tq=256):
    b, s, _ = rope3.shape
    nq = s // tq
    nblk = s // WINDOW
    per = tq // WINDOW
    ka_blk = A_Q // A_KV
    prev_map = lambda bi, i: (bi, jnp.maximum(i * per - 1, 0), ka_blk)
    cur_map = lambda bi, i: (bi, i, ka_blk)
    next_map = lambda bi, i: (bi, jnp.minimum((i + 1) * per, nblk - 1), ka_blk)
    vprev_map = lambda bi, i: (bi, jnp.maximum(i * per - 1, 0), 0)
    vcur_map = lambda bi, i: (bi, i, 0)
    vnext_map = lambda bi, i: (bi, jnp.minimum((i + 1) * per, nblk - 1), 0)
    return pl.pallas_call(
        functools.partial(_attn_a_body, tq=tq, nq=nq),
        grid=(b, nq),
        in_specs=[
            pl.BlockSpec(memory_space=pltpu.SMEM),
            pl.BlockSpec((1, tq, A_Q), lambda bi, i: (bi, i, 0)),
            pl.BlockSpec((1, WINDOW, A_KV), prev_map),
            pl.BlockSpec((1, tq, A_KV), cur_map),
            pl.BlockSpec((1, WINDOW, A_KV), next_map),
            pl.BlockSpec((1, WINDOW, A_KV), vprev_map),
            pl.BlockSpec((1, tq, A_KV), vcur_map),
            pl.BlockSpec((1, WINDOW, A_KV), vnext_map),
        ],
        out_specs=pl.BlockSpec((1, tq, A_Q), lambda bi, i: (bi, i, 0)),
        out_shape=jax.ShapeDtypeStruct((b, s, A_Q), BF16),
        compiler_params=pltpu.CompilerParams(
            dimension_semantics=("parallel", "parallel"), vmem_limit_bytes=VMEM_LIMIT),
        name="attn_a",
    )(sink, rope3, rope3, rope3, rope3, v3, v3, v3)


def _attn_b_body(q_ref, k_ref, v_ref, lam_ref, g_ref, o_ref,
                 s_scr, mx_scr, p_scr, a_scr, m_scr, l_scr, acc_scr, kn_scr, *, tk, nk, lam_init):
    reps = tk // LANES
    tq = q_ref.shape[1]

    def key_chunk(c):
        return k_ref[0, pl.ds(pl.multiple_of(c * tk, tk), tk), :]

    def val_chunk(c):
        return v_ref[0, pl.ds(pl.multiple_of(c * tk, tk), tk), :]

    def qk(c, comp):
        q = q_ref[0, :, comp * HEAD_DIM:(comp + 1) * HEAD_DIM]
        k = key_chunk(c)[:, comp * HEAD_DIM:(comp + 1) * HEAD_DIM]
        return lax.dot_general(q, k, (((1,), (1,)), ((), ())), preferred_element_type=F32)

    def lane_fold(x, op):
        y = x[:, :LANES]
        for j in range(1, reps):
            y = op(y, x[:, j * LANES:(j + 1) * LANES])
        return y

    @pl.when(pl.program_id(2) == 0)
    def _():
        def chunk_max(c, carry):
            kc = key_chunk(c).astype(F32)
            sq = kc * kc
            out = []
            for comp in range(2):
                n2 = jnp.sum(sq[:, comp * HEAD_DIM:(comp + 1) * HEAD_DIM], axis=-1, keepdims=True)
                out.append(jnp.maximum(carry[comp], jnp.max(n2, axis=0, keepdims=True)))
            return tuple(out)
        zero = jnp.zeros((1, 1), F32)
        n2max = lax.fori_loop(0, nk, chunk_max, (zero, zero))
        for comp in range(2):
            kn_scr[comp] = jnp.broadcast_to(jnp.sqrt(n2max[comp]), kn_scr.shape[1:])

    def scores(c, slot):
        for comp in range(2):
            s = qk(c, comp)
            s_scr[slot, comp] = s
            mx_scr[slot, comp] = lane_fold(s, jnp.maximum)

    def softmax(slot):
        for comp in range(2):
            m_prev = m_scr[comp]
            m_new = jnp.maximum(m_prev, jnp.max(mx_scr[slot, comp], axis=-1, keepdims=True))
            a_scr[slot, comp] = jnp.exp2(m_prev - m_new)
            m_scr[comp] = m_new
            e = jnp.exp2(s_scr[slot, comp] - _lane_tile(m_new, reps))
            l_scr[comp] = a_scr[slot, comp] * l_scr[comp] + lane_fold(e, jnp.add)
            p_scr[slot, comp] = e.astype(BF16)

    def pv(c, slot):
        vc = val_chunk(c)
        for comp in range(2):
            acc_scr[comp] = _lane_tile(a_scr[slot, comp], 2) * acc_scr[comp] + jnp.dot(
                p_scr[slot, comp], vc, preferred_element_type=F32)

    def steady_pair(i):
        pv(i - 1, 0)
        scores(i + 1, 0)
        softmax(1)
        pv(i, 1)
        scores(i + 2, 1)
        softmax(0)

    def steady_quad(j, carry):
        steady_pair(4 * j + 1)
        steady_pair(4 * j + 3)
        return carry

    def probs(c, slot, s_of):
        for comp in range(2):
            e = jnp.exp2(s_of(comp) - _lane_tile(m_scr[comp], reps))
            l_scr[comp] += lane_fold(e, jnp.add)
            p_scr[slot, comp] = e.astype(BF16)

    def pv_plain(c, slot):
        vc = val_chunk(c)
        for comp in range(2):
            acc_scr[comp] += jnp.dot(p_scr[slot, comp], vc, preferred_element_type=F32)

    def fixed_group(t, carry):
        for r in range(FIXED_GROUP):
            c = 2 + FIXED_GROUP * t + r
            pv_plain(c - 2, r % 2)
            probs(c, r % 2, lambda comp: qk(c, comp))
        return carry

    l_scr[...] = jnp.zeros_like(l_scr)
    acc_scr[...] = jnp.zeros_like(acc_scr)
    scores(0, 0)
    scores(1, 1)
    span = jnp.zeros((1, 1), F32)
    for comp in range(2):
        qf = q_ref[0, :, comp * HEAD_DIM:(comp + 1) * HEAD_DIM].astype(F32)
        q_norm = jnp.sqrt(jnp.sum(qf * qf, axis=-1, keepdims=True))
        bound = q_norm * kn_scr[comp][:1, :1]
        first_max = jnp.max(mx_scr[0, comp], axis=-1, keepdims=True)
        span = jnp.maximum(span, jnp.max(bound - first_max, axis=0, keepdims=True))
        m_scr[comp] = jnp.broadcast_to(first_max, (tq, LANES))
    use_fixed = span[0, 0] <= FIXED_SHIFT_SPAN
    probs(0, 0, lambda comp: s_scr[0, comp])

    @pl.when(use_fixed)
    def _():
        probs(1, 1, lambda comp: s_scr[1, comp])
        lax.fori_loop(0, (nk - 2) // FIXED_GROUP, fixed_group, 0)
        pv_plain(nk - 2, 0)
        pv_plain(nk - 1, 1)

    @pl.when(jnp.logical_not(use_fixed))
    def _():
        m_scr[...] = jnp.full_like(m_scr, NEG_INF)
        l_scr[...] = jnp.zeros_like(l_scr)
        softmax(0)
        n_pairs = (nk - 2) // 2
        lax.fori_loop(0, n_pairs // 2, steady_quad, 0)
        if n_pairs % 2:
            steady_pair(nk - 3)
        softmax(1)
        pv(nk - 2, 0)
        pv(nk - 1, 1)

    lam_v = lam_ref[...]
    lam = (jnp.exp(jnp.sum(lam_v[0:1] * lam_v[1:2], axis=-1, keepdims=True))
           - jnp.exp(jnp.sum(lam_v[2:3] * lam_v[3:4], axis=-1, keepdims=True))
           + lam_init)
    l0 = jnp.sum(l_scr[0], axis=-1, keepdims=True)
    l1 = jnp.sum(l_scr[1], axis=-1, keepdims=True)
    o = acc_scr[0] / l0 - lam * (acc_scr[1] / l1)
    o_ref[0] = (_rms(o, g_ref[...]) * (1.0 - lam_init)).astype(BF16)


def _attn_b(rope3, v3, lam_vecs, subln_g, lam_init, *, tq=512, tk=512):
    b, s, _ = rope3.shape
    nk = s // tk
    assert nk % 2 == 0 and nk >= 4 and (nk - 2) % FIXED_GROUP == 0 and FIXED_GROUP % 2 == 0
    w = 2 * HEAD_DIM
    qb_blk = (A_Q + A_KV) // w
    kb_blk = qb_blk + B_HEADS
    vb_blk = A_KV // w
    return pl.pallas_call(
        functools.partial(_attn_b_body, tk=tk, nk=nk, lam_init=lam_init),
        grid=(b, B_HEADS, s // tq),
        in_specs=[
            pl.BlockSpec((1, tq, w), lambda bi, h, i: (bi, i, qb_blk + h)),
            pl.BlockSpec((1, s, w), lambda bi, h, i: (bi, 0, kb_blk + h)),
            pl.BlockSpec((1, s, w), lambda bi, h, i: (bi, 0, vb_blk + h)),
            pl.BlockSpec((4, HEAD_DIM), lambda bi, h, i: (0, 0)),
            pl.BlockSpec((1, w), lambda bi, h, i: (0, 0)),
        ],
        out_specs=pl.BlockSpec((1, tq, w), lambda bi, h, i: (bi, i, h)),
        out_shape=jax.ShapeDtypeStruct((b, s, B_V), BF16),
        scratch_shapes=[pltpu.VMEM((2, 2, tq, tk), F32),
                        pltpu.VMEM((2, 2, tq, LANES), F32),
                        pltpu.VMEM((2, 2, tq, tk), BF16),
                        pltpu.VMEM((2, 2, tq, LANES), F32),
                        pltpu.VMEM((2, tq, LANES), F32),
                        pltpu.VMEM((2, tq, LANES), F32),
                        pltpu.VMEM((2, tq, w), F32),
                        pltpu.VMEM((2, 8, LANES), F32)],
        compiler_params=pltpu.CompilerParams(
            dimension_semantics=("parallel", "parallel", "arbitrary"),
            vmem_limit_bytes=VMEM_LIMIT),
        name="attn_b",
    )(rope3, rope3, v3, lam_vecs, subln_g)


def _merge_body(oa_ref, ob_ref, gate_ref, x_ref, wa_ref, wb_ref, wo_ref, g_ref, o_ref):
    pa = jnp.dot(oa_ref[...], wa_ref[...], preferred_element_type=F32)
    pb = jnp.dot(ob_ref[...], wb_ref[...], preferred_element_type=F32)
    ga = gate_ref[:, :D_MODEL].astype(F32)
    gb = gate_ref[:, D_MODEL:].astype(F32)
    merged = (ga * pa + gb * pb).astype(BF16)
    y = jnp.dot(merged, wo_ref[...], preferred_element_type=F32)
    o_ref[...] = x_ref[...] + _rms(y, g_ref[...])


def _merge(oa, ob, gates, x1, wa, wb, wo, g_post, *, tm=512):
    t, d = x1.shape
    rows = lambda n: pl.BlockSpec((tm, n), lambda i: (i, 0))
    return pl.pallas_call(
        _merge_body,
        grid=(t // tm,),
        in_specs=[rows(A_Q), rows(B_V), rows(GATE_COLS), rows(d),
                  _resident(wa.shape), _resident(wb.shape), _resident(wo.shape),
                  pl.BlockSpec((1, d), lambda i: (0, 0))],
        out_specs=rows(d),
        out_shape=jax.ShapeDtypeStruct((t, d), F32),
        compiler_params=pltpu.CompilerParams(
            dimension_semantics=("parallel",), vmem_limit_bytes=VMEM_LIMIT),
        name="merge",
    )(oa, ob, gates, x1, wa, wb, wo, g_post)


def _rope_tables(seq):
    pos = jnp.arange(seq, dtype=F32)
    inv = ROPE_THETA ** (-jnp.arange(0, HEAD_DIM, 2, dtype=F32) / HEAD_DIM)
    ang = pos[:, None] * inv[None, :]
    cos, sin = jnp.cos(ang), jnp.sin(ang)
    return jnp.concatenate([cos, cos], axis=-1), jnp.concatenate([-sin, sin], axis=-1)


def kernel(x, ffn1_pre_g, ffn1_w_gate, ffn1_w_up, ffn1_w_down, ffn1_post_g, mix_pre_g, w_in, gate_bias, sink_logit, lambda_q1, lambda_k1, lambda_q2, lambda_k2, subln_g, w_proj_a, w_proj_b, w_out, mix_post_g, ffn2_pre_g, ffn2_w_gate, ffn2_w_up, ffn2_w_down, ffn2_post_g):
    b, s, d = x.shape
    t = b * s
    depth = ffn1_pre_g.shape[0]
    cos2, sin2 = _rope_tables(s)
    xt = x.reshape(t, d)
    q_scaled = (QK_SCALE,) * A_Q_HEADS + (1.0,) * A_KV_HEADS + (QK_SCALE,) * (2 * B_HEADS) + (1.0,) * (2 * B_HEADS)
    tm_proj = 1024
    for l in range(depth):
        lam_init = 0.8 - 0.6 * math.exp(-0.3 * l)
        row = lambda p: p[l:l + 1]
        bf = lambda p: p[l].astype(BF16)

        xt, h = _ffn(xt, row(ffn1_pre_g), bf(ffn1_w_gate), bf(ffn1_w_up), bf(ffn1_w_down),
                     row(ffn1_post_g), row(mix_pre_g))

        w = w_in[l]
        qa_end, ka_end, va_end = A_Q, A_Q + A_KV, A_Q + 2 * A_KV
        qb_end = va_end + B_QK
        kb_end = qb_end + B_QK
        vb_end = kb_end + B_V
        w_rope = jnp.concatenate([w[:, :ka_end], w[:, va_end:kb_end]], axis=1).astype(BF16)
        w_v = jnp.concatenate([w[:, ka_end:va_end], w[:, kb_end:vb_end]], axis=1).astype(BF16)
        w_gate = w[:, vb_end:].astype(BF16)

        ns = s // tm_proj
        tab = pl.BlockSpec((tm_proj, HEAD_DIM), lambda i: (i % ns, 0))
        rope = _proj_call(functools.partial(_proj_rope_body, head_scales=q_scaled),
                          h, w_rope, [cos2, sin2], [tab, tab], ROPE_COLS, "proj_rope", tm=tm_proj)
        vals = _proj_call(functools.partial(_proj_plain_body, chunk=MXU_N),
                          h, w_v, [], [], V_COLS, "proj_v", tm=tm_proj)
        gates = _proj_call(functools.partial(_proj_gate_body, chunk=2 * MXU_N),
                           h, w_gate, [gate_bias[l:l + 1]], [_resident((1, GATE_COLS))],
                           GATE_COLS, "proj_gate", tm=tm_proj)

        rope3 = rope.reshape(b, s, ROPE_COLS)
        v3 = vals.reshape(b, s, V_COLS)
        out_a = _attn_a(rope3, v3, sink_logit[l])
        lam_vecs = jnp.stack([lambda_q1[l], lambda_k1[l], lambda_q2[l], lambda_k2[l]]).astype(F32)
        out_b = _attn_b(rope3, v3, lam_vecs, row(subln_g), lam_init)

        xt = _merge(out_a.reshape(t, A_Q), out_b.reshape(t, B_V), gates, xt,
                    bf(w_proj_a), bf(w_proj_b), bf(w_out), row(mix_post_g))

        xt = _ffn(xt, row(ffn2_pre_g), bf(ffn2_w_gate), bf(ffn2_w_up), bf(ffn2_w_down),
                  row(ffn2_post_g))
    return xt.reshape(b, s, d)
```

```python
import functools
import math

import jax
import jax.numpy as jnp
from jax import lax
from jax.experimental import pallas as pl
from jax.experimental.pallas import tpu as pltpu

D_MODEL = 2048
HEAD_DIM = 128
A_Q_HEADS = 8
A_KV_HEADS = 2
A_REP = A_Q_HEADS // A_KV_HEADS
WINDOW = 128
B_HEADS = 4
D_FF = 5632
ROPE_THETA = 10000.0
EPS = 1e-6
NEG_INF = -1e30
LOG2E = 1.4426950408889634
QK_SCALE = HEAD_DIM ** -0.5 * LOG2E
FIXED_SHIFT_SPAN = 60.0
FIXED_GROUP = 30

A_Q = A_Q_HEADS * HEAD_DIM
A_KV = A_KV_HEADS * HEAD_DIM
B_QK = B_HEADS * 2 * HEAD_DIM
B_V = B_HEADS * 2 * HEAD_DIM
ROPE_COLS = A_Q + A_KV + 2 * B_QK
V_COLS = A_KV + B_V
GATE_COLS = 2 * D_MODEL

LANES = 128
MXU_N = 256
VMEM_LIMIT = 56 * 1024 * 1024

F32 = jnp.float32
BF16 = jnp.bfloat16


def _rms(x, g):
    ms = jnp.mean(x * x, axis=-1, keepdims=True)
    return x * lax.rsqrt(ms + EPS) * g


def _lane_tile(x, n):
    return jnp.concatenate([x] * n, axis=1)


def _resident(shape):
    return pl.BlockSpec(shape, lambda *_: (0,) * len(shape), pipeline_mode=pl.Buffered(1))


def _ffn_body(*refs, nf, tf, n_sub_last, emit_next):
    if emit_next:
        (x_ref, gpre_ref, wg_ref, wu_ref, wd_ref, gpost_ref, gnext_ref,
         o_ref, hn_ref, h_scr) = refs
    else:
        (x_ref, gpre_ref, wg_ref, wu_ref, wd_ref, gpost_ref, o_ref, h_scr) = refs
    f = pl.program_id(1)
    n_sub = wg_ref.shape[1] // tf

    def sub_blocks(n, h, first=False):
        for j in range(n):
            cols = slice(j * tf, (j + 1) * tf)
            gate = jnp.dot(h, wg_ref[:, cols], preferred_element_type=F32)
            up = jnp.dot(h, wu_ref[:, cols], preferred_element_type=F32)
            a = gate * (1.0 / (1.0 + jnp.exp(-gate))) * up
            y = jnp.dot(a.astype(BF16), wd_ref[cols, :], preferred_element_type=F32)
            if first and j == 0:
                o_ref[...] = y
            else:
                o_ref[...] += y

    @pl.when(f == 0)
    def _():
        h = _rms(x_ref[...], gpre_ref[...]).astype(BF16)
        h_scr[...] = h
        sub_blocks(n_sub, h, first=True)

    @pl.when((f > 0) & (f < nf - 1))
    def _():
        sub_blocks(n_sub, h_scr[...])

    @pl.when(f == nf - 1)
    def _():
        sub_blocks(n_sub_last, h_scr[...])
        xo = x_ref[...] + 0.5 * _rms(o_ref[...], gpost_ref[...])
        o_ref[...] = xo
        if emit_next:
            hn_ref[...] = _rms(xo, gnext_ref[...]).astype(BF16)


def _ffn(x, g_pre, wg, wu, wd, g_post, g_next=None, *, tm=512, tf=512, n_sub=2):
    t, d = x.shape
    blk = tf * n_sub
    nf = pl.cdiv(D_FF, blk)
    n_sub_last = (D_FF - (nf - 1) * blk) // tf
    assert (nf - 1) * blk + n_sub_last * tf == D_FF
    emit_next = g_next is not None
    row = pl.BlockSpec((tm, d), lambda i, f: (i, 0))
    vec = pl.BlockSpec((1, d), lambda i, f: (0, 0))
    in_specs = [row, vec,
                pl.BlockSpec((d, blk), lambda i, f: (0, f)),
                pl.BlockSpec((d, blk), lambda i, f: (0, f)),
                pl.BlockSpec((blk, d), lambda i, f: (f, 0)),
                vec]
    args = [x, g_pre, wg, wu, wd, g_post]
    out_shape = [jax.ShapeDtypeStruct((t, d), F32)]
    out_specs = [row]
    if emit_next:
        in_specs.append(vec)
        args.append(g_next)
        out_shape.append(jax.ShapeDtypeStruct((t, d), BF16))
        out_specs.append(row)
    res = pl.pallas_call(
        functools.partial(_ffn_body, nf=nf, tf=tf, n_sub_last=n_sub_last, emit_next=emit_next),
        grid=(t // tm, nf),
        in_specs=in_specs,
        out_specs=out_specs,
        out_shape=out_shape,
        scratch_shapes=[pltpu.VMEM((tm, d), BF16)],
        compiler_params=pltpu.CompilerParams(
            dimension_semantics=("parallel", "arbitrary"), vmem_limit_bytes=VMEM_LIMIT),
        name="ffn_next" if emit_next else "ffn",
    )(*args)
    return res if emit_next else res[0]


def _proj_rope_body(h_ref, w_ref, cos_ref, sin_ref, o_ref, *, head_scales):
    h = h_ref[...]
    cos = cos_ref[...]
    sin = sin_ref[...]
    heads_per_chunk = MXU_N // HEAD_DIM
    for c in range(len(head_scales) // heads_per_chunk):
        acc = jnp.dot(h, w_ref[:, c * MXU_N:(c + 1) * MXU_N], preferred_element_type=F32)
        for k in range(heads_per_chunk):
            hd = c * heads_per_chunk + k
            a = acc[:, k * HEAD_DIM:(k + 1) * HEAD_DIM]
            r = a * cos + pltpu.roll(a, HEAD_DIM // 2, 1) * sin
            if head_scales[hd] != 1.0:
                r = r * head_scales[hd]
            o_ref[:, hd * HEAD_DIM:(hd + 1) * HEAD_DIM] = r.astype(BF16)


def _proj_plain_body(h_ref, w_ref, o_ref, *, chunk):
    h = h_ref[...]
    for c in range(w_ref.shape[1] // chunk):
        acc = jnp.dot(h, w_ref[:, c * chunk:(c + 1) * chunk], preferred_element_type=F32)
        o_ref[:, c * chunk:(c + 1) * chunk] = acc.astype(BF16)


def _proj_gate_body(h_ref, w_ref, b_ref, o_ref, *, chunk):
    h = h_ref[...]
    for c in range(w_ref.shape[1] // chunk):
        acc = jnp.dot(h, w_ref[:, c * chunk:(c + 1) * chunk], preferred_element_type=F32)
        z = acc + b_ref[:, c * chunk:(c + 1) * chunk]
        o_ref[:, c * chunk:(c + 1) * chunk] = (1.0 / (1.0 + jnp.exp(-z))).astype(BF16)


def _proj_call(body, h, w, extra, extra_specs, n_out, name, *, tm):
    t, d = h.shape
    return pl.pallas_call(
        body,
        grid=(t // tm,),
        in_specs=[pl.BlockSpec((tm, d), lambda i: (i, 0)), _resident(w.shape)] + extra_specs,
        out_specs=pl.BlockSpec((tm, n_out), lambda i: (i, 0)),
        out_shape=jax.ShapeDtypeStruct((t, n_out), BF16),
        compiler_params=pltpu.CompilerParams(
            dimension_semantics=("parallel",), vmem_limit_bytes=VMEM_LIMIT),
        name=name,
    )(h, w, *extra)


def _attn_a_body(sink_ref, q_ref, kp_ref, kc_ref, kn_ref, vp_ref, vc_ref, vn_ref, o_ref, *, tq, nq):
    i = pl.program_id(1)
    nk = tq + 2 * WINDOW
    row = lax.broadcasted_iota(jnp.int32, (tq, nk), 0)
    col = lax.broadcasted_iota(jnp.int32, (tq, nk), 1)
    rel = col - WINDOW - row
    valid = (rel >= -WINDOW) & (rel <= WINDOW)
    valid = valid & ((col >= WINDOW) | (i > 0))
    valid = valid & ((col < tq + WINDOW) | (i < nq - 1))
    k_all = jnp.concatenate([kp_ref[0], kc_ref[0], kn_ref[0]], axis=0)
    v_all = jnp.concatenate([vp_ref[0], vc_ref[0], vn_ref[0]], axis=0)
    for g in range(A_KV_HEADS):
        k = k_all[:, g * HEAD_DIM:(g + 1) * HEAD_DIM]
        v = v_all[:, g * HEAD_DIM:(g + 1) * HEAD_DIM]
        for r in range(A_REP):
            hd = g * A_REP + r
            q = q_ref[0, :, hd * HEAD_DIM:(hd + 1) * HEAD_DIM]
            s = lax.dot_general(q, k, (((1,), (1,)), ((), ())), preferred_element_type=F32)
            s = jnp.where(valid, s, NEG_INF)
            sk = sink_ref[hd] * LOG2E
            m = jnp.maximum(jnp.max(s, axis=-1, keepdims=True), sk)
            e = jnp.exp2(s - m)
            den = jnp.sum(e, axis=-1, keepdims=True) + jnp.exp2(sk - m)
            o = jnp.dot(e.astype(BF16), v, preferred_element_type=F32) / den
            o_ref[0, :, hd * HEAD_DIM:(hd + 1) * HEAD_DIM] = o.astype(BF16)


def _attn_a(rope3, v3, sink, *, tq=512):
    b, s, _ = rope3.shape
    nq = s // tq
    nblk = s // WINDOW
    per = tq // WINDOW
    ka_blk = A_Q // A_KV
    prev_map = lambda bi, i: (bi, jnp.maximum(i * per - 1, 0), ka_blk)
    cur_map = lambda bi, i: (bi, i, ka_blk)
    next_map = lambda bi, i: (bi, jnp.minimum((i + 1) * per, nblk - 1), ka_blk)
    vprev_map = lambda bi, i: (bi, jnp.maximum(i * per - 1, 0), 0)
    vcur_map = lambda bi, i: (bi, i, 0)
    vnext_map = lambda bi, i: (bi, jnp.minimum((i + 1) * per, nblk - 1), 0)
    return pl.pallas_call(
        functools.partial(_attn_a_body, tq=tq, nq=nq),
        grid=(b, nq),
        in_specs=[
            pl.BlockSpec(memory_space=pltpu.SMEM),
            pl.BlockSpec((1, tq, A_Q), lambda bi, i: (bi, i, 0)),
            pl.BlockSpec((1, WINDOW, A_KV), prev_map),
            pl.BlockSpec((1, tq, A_KV), cur_map),
            pl.BlockSpec((1, WINDOW, A_KV), next_map),
            pl.BlockSpec((1, WINDOW, A_KV), vprev_map),
            pl.BlockSpec((1, tq, A_KV), vcur_map),
            pl.BlockSpec((1, WINDOW, A_KV), vnext_map),
        ],
        out_specs=pl.BlockSpec((1, tq, A_Q), lambda bi, i: (bi, i, 0)),
        out_shape=jax.ShapeDtypeStruct((b, s, A_Q), BF16),
        compiler_params=pltpu.CompilerParams(
            dimension_semantics=("parallel", "parallel"), vmem_limit_bytes=VMEM_LIMIT),
        name="attn_a",
    )(sink, rope3, rope3, rope3, rope3, v3, v3, v3)


def _attn_b_body(q_ref, k_ref, v_ref, lam_ref, g_ref, o_ref,
                 s_scr, mx_scr, p_scr, a_scr, m_scr, l_scr, acc_scr, kn_scr, *, tk, nk, lam_init):
    reps = tk // LANES
    tq = q_ref.shape[1]

    def key_chunk(c):
        return k_ref[0, pl.ds(pl.multiple_of(c * tk, tk), tk), :]

    def val_chunk(c):
        return v_ref[0, pl.ds(pl.multiple_of(c * tk, tk), tk), :]

    def qk(c, comp):
        q = q_ref[0, :, comp * HEAD_DIM:(comp + 1) * HEAD_DIM]
        k = key_chunk(c)[:, comp * HEAD_DIM:(comp + 1) * HEAD_DIM]
        return lax.dot_general(q, k, (((1,), (1,)), ((), ())), preferred_element_type=F32)

    def lane_fold(x, op):
        y = x[:, :LANES]
        for j in range(1, reps):
            y = op(y, x[:, j * LANES:(j + 1) * LANES])
        return y

    @pl.when(pl.program_id(2) == 0)
    def _():
        def chunk_max(c, carry):
            kc = key_chunk(c).astype(F32)
            sq = kc * kc
            out = []
            for comp in range(2):
                n2 = jnp.sum(sq[:, comp * HEAD_DIM:(comp + 1) * HEAD_DIM], axis=-1, keepdims=True)
                out.append(jnp.maximum(carry[comp], jnp.max(n2, axis=0, keepdims=True)))
            return tuple(out)
        zero = jnp.zeros((1, 1), F32)
        n2max = lax.fori_loop(0, nk, chunk_max, (zero, zero))
        for comp in range(2):
            kn_scr[comp] = jnp.broadcast_to(jnp.sqrt(n2max[comp]), kn_scr.shape[1:])

    def scores(c, slot):
        for comp in range(2):
            s = qk(c, comp)
            s_scr[slot, comp] = s
            mx_scr[slot, comp] = lane_fold(s, jnp.maximum)

    def softmax(slot):
        for comp in range(2):
            m_prev = m_scr[comp]
            m_new = jnp.maximum(m_prev, jnp.max(mx_scr[slot, comp], axis=-1, keepdims=True))
            a_scr[slot, comp] = jnp.exp2(m_prev - m_new)
            m_scr[comp] = m_new
            e = jnp.exp2(s_scr[slot, comp] - _lane_tile(m_new, reps))
            l_scr[comp] = a_scr[slot, comp] * l_scr[comp] + lane_fold(e, jnp.add)
            p_scr[slot, comp] = e.astype(BF16)

    def pv(c, slot):
        vc = val_chunk(c)
        for comp in range(2):
            acc_scr[comp] = _lane_tile(a_scr[slot, comp], 2) * acc_scr[comp] + jnp.dot(
                p_scr[slot, comp], vc, preferred_element_type=F32)

    def steady_pair(i):
        pv(i - 1, 0)
        scores(i + 1, 0)
        softmax(1)
        pv(i, 1)
        scores(i + 2, 1)
        softmax(0)

    def steady_quad(j, carry):
        steady_pair(4 * j + 1)
        steady_pair(4 * j + 3)
        return carry

    def probs(c, slot, s_of):
        for comp in range(2):
            e = jnp.exp2(s_of(comp) - _lane_tile(m_scr[comp], reps))
            l_scr[comp] += lane_fold(e, jnp.add)
            p_scr[slot, comp] = e.astype(BF16)

    def pv_plain(c, slot):
        vc = val_chunk(c)
        for comp in range(2):
            acc_scr[comp] += jnp.dot(p_scr[slot, comp], vc, preferred_element_type=F32)

    def fixed_group(t, carry):
        for r in range(FIXED_GROUP):
            c = 2 + FIXED_GROUP * t + r
            pv_plain(c - 2, r % 2)
            probs(c, r % 2, lambda comp: qk(c, comp))
        return carry

    l_scr[...] = jnp.zeros_like(l_scr)
    acc_scr[...] = jnp.zeros_like(acc_scr)
    scores(0, 0)
    scores(1, 1)
    span = jnp.zeros((1, 1), F32)
    for comp in range(2):
        qf = q_ref[0, :, comp * HEAD_DIM:(comp + 1) * HEAD_DIM].astype(F32)
        q_norm = jnp.sqrt(jnp.sum(qf * qf, axis=-1, keepdims=True))
        bound = q_norm * kn_scr[comp][:1, :1]
        first_max = jnp.max(mx_scr[0, comp], axis=-1, keepdims=True)
        span = jnp.maximum(span, jnp.max(bound - first_max, axis=0, keepdims=True))
        m_scr[comp] = jnp.broadcast_to(first_max, (tq, LANES))
    use_fixed = span[0, 0] <= FIXED_SHIFT_SPAN
    probs(0, 0, lambda comp: s_scr[0, comp])

    @pl.when(use_fixed)
    def _():
        probs(1, 1, lambda comp: s_scr[1, comp])
        lax.fori_loop(0, (nk - 2) // FIXED_GROUP, fixed_group, 0)
        pv_plain(nk - 2, 0)
        pv_plain(nk - 1, 1)

    @pl.when(jnp.logical_not(use_fixed))
    def _():
        m_scr[...] = jnp.full_like(m_scr, NEG_INF)
        l_scr[...] = jnp.zeros_like(l_scr)
        softmax(0)
        n_pairs = (nk - 2) // 2
        lax.fori_loop(0, n_pairs // 2, steady_quad, 0)
        if n_pairs % 2:
            steady_pair(nk - 3)
        softmax(1)
        pv(nk - 2, 0)
        pv(nk - 1, 1)

    lam_v = lam_ref[...]
    lam = (jnp.exp(jnp.sum(lam_v[0:1] * lam_v[1:2], axis=-1, keepdims=True))
           - jnp.exp(jnp.sum(lam_v[2:3] * lam_v[3:4], axis=-1, keepdims=True))
           + lam_init)
    l0 = jnp.sum(l_scr[0], axis=-1, keepdims=True)
    l1 = jnp.sum(l_scr[1], axis=-1, keepdims=True)
    o = acc_scr[0] / l0 - lam * (acc_scr[1] / l1)
    o_ref[0] = (_rms(o, g_ref[...]) * (1.0 - lam_init)).astype(BF16)


def _attn_b(rope3, v3, lam_vecs, subln_g, lam_init, *, tq=512, tk=512):
    b, s, _ = rope3.shape
    nk = s // tk
    assert nk % 2 == 0 and nk >= 4 and (nk - 2) % FIXED_GROUP == 0 and FIXED_GROUP % 2 == 0
    w = 2 * HEAD_DIM
    qb_blk = (A_Q + A_KV) // w
    kb_blk = qb_blk + B_HEADS
    vb_blk = A_KV // w
    return pl.pallas_call(
        functools.partial(_attn_b_body, tk=tk, nk=nk, lam_init=lam_init),
        grid=(b, B_HEADS, s // tq),
        in_specs=[
            pl.BlockSpec((1, tq, w), lambda bi, h, i: (bi, i, qb_blk + h)),
            pl.BlockSpec((1, s, w), lambda bi, h, i: (bi, 0, kb_blk + h)),
            pl.BlockSpec((1, s, w), lambda bi, h, i: (bi, 0, vb_blk + h)),
            pl.BlockSpec((4, HEAD_DIM), lambda bi, h, i: (0, 0)),
            pl.BlockSpec((1, w), lambda bi, h, i: (0, 0)),
        ],
        out_specs=pl.BlockSpec((1, tq, w), lambda bi, h, i: (bi, i, h)),
        out_shape=jax.ShapeDtypeStruct((b, s, B_V), BF16),
        scratch_shapes=[pltpu.VMEM((2, 2, tq, tk), F32),
                        pltpu.VMEM((2, 2, tq, LANES), F32),
                        pltpu.VMEM((2, 2, tq, tk), BF16),
                        pltpu.VMEM((2, 2, tq, LANES), F32),
                        pltpu.VMEM((2, tq, LANES), F32),
                        pltpu.VMEM((2, tq, LANES), F32),
                        pltpu.VMEM((2, tq, w), F32),
                        pltpu.VMEM((2, 8, LANES), F32)],
        compiler_params=pltpu.CompilerParams(
            dimension_semantics=("parallel", "parallel", "arbitrary"),
            vmem_limit_bytes=VMEM_LIMIT),
        name="attn_b",
    )(rope3, rope3, v3, lam_vecs, subln_g)


def _merge_body(oa_ref, ob_ref, gate_ref, x_ref, wa_ref, wb_ref, wo_ref, g_ref, o_ref):
    pa = jnp.dot(oa_ref[...], wa_ref[...], preferred_element_type=F32)
    pb = jnp.dot(ob_ref[...], wb_ref[...], preferred_element_type=F32)
    ga = gate_ref[:, :D_MODEL].astype(F32)
    gb = gate_ref[:, D_MODEL:].astype(F32)
    merged = (ga * pa + gb * pb).astype(BF16)
    y = jnp.dot(merged, wo_ref[...], preferred_element_type=F32)
    o_ref[...] = x_ref[...] + _rms(y, g_ref[...])


def _merge(oa, ob, gates, x1, wa, wb, wo, g_post, *, tm=512):
    t, d = x1.shape
    rows = lambda n: pl.BlockSpec((tm, n), lambda i: (i, 0))
    return pl.pallas_call(
        _merge_body,
        grid=(t // tm,),
        in_specs=[rows(A_Q), rows(B_V), rows(GATE_COLS), rows(d),
                  _resident(wa.shape), _resident(wb.shape), _resident(wo.shape),
                  pl.BlockSpec((1, d), lambda i: (0, 0))],
        out_specs=rows(d),
        out_shape=jax.ShapeDtypeStruct((t, d), F32),
        compiler_params=pltpu.CompilerParams(
            dimension_semantics=("parallel",), vmem_limit_bytes=VMEM_LIMIT),
        name="merge",
    )(oa, ob, gates, x1, wa, wb, wo, g_post)


def _rope_tables(seq):
    pos = jnp.arange(seq, dtype=F32)
    inv = ROPE_THETA ** (-jnp.arange(0, HEAD_DIM, 2, dtype=F32) / HEAD_DIM)
    ang = pos[:, None] * inv[None, :]
    cos, sin = jnp.cos(ang), jnp.sin(ang)
    return jnp.concatenate([cos, cos], axis=-1), jnp.concatenate([-sin, sin], axis=-1)


def kernel(x, ffn1_pre_g, ffn1_w_gate, ffn1_w_up, ffn1_w_down, ffn1_post_g, mix_pre_g, w_in, gate_bias, sink_logit, lambda_q1, lambda_k1, lambda_q2, lambda_k2, subln_g, w_proj_a, w_proj_b, w_out, mix_post_g, ffn2_pre_g, ffn2_w_gate, ffn2_w_up, ffn2_w_down, ffn2_post_g):
    b, s, d = x.shape
    t = b * s
    depth = ffn1_pre_g.shape[0]
    cos2, sin2 = _rope_tables(s)
    xt = x.reshape(t, d)
    q_scaled = (QK_SCALE,) * A_Q_HEADS + (1.0,) * A_KV_HEADS + (QK_SCALE,) * (2 * B_HEADS) + (1.0,) * (2 * B_HEADS)
    tm_proj = 1024
    for l in range(depth):
        lam_init = 0.8 - 0.6 * math.exp(-0.3 * l)
        row = lambda p: p[l:l + 1]
        bf = lambda p: p[l].astype(BF16)

        xt, h = _ffn(xt, row(ffn1_pre_g), bf(ffn1_w_gate), bf(ffn1_w_up), bf(ffn1_w_down),
                     row(ffn1_post_g), row(mix_pre_g))

        w = w_in[l]
        qa_end, ka_end, va_end = A_Q, A_Q + A_KV, A_Q + 2 * A_KV
        qb_end = va_end + B_QK
        kb_end = qb_end + B_QK
        vb_end = kb_end + B_V
        w_rope = jnp.concatenate([w[:, :ka_end], w[:, va_end:kb_end]], axis=1).astype(BF16)
        w_v = jnp.concatenate([w[:, ka_end:va_end], w[:, kb_end:vb_end]], axis=1).astype(BF16)
        w_gate = w[:, vb_end:].astype(BF16)

        ns = s // tm_proj
        tab = pl.BlockSpec((tm_proj, HEAD_DIM), lambda i: (i % ns, 0))
        rope = _proj_call(functools.partial(_proj_rope_body, head_scales=q_scaled),
                          h, w_rope, [cos2, sin2], [tab, tab], ROPE_COLS, "proj_rope", tm=tm_proj)
        vals = _proj_call(functools.partial(_proj_plain_body, chunk=MXU_N),
                          h, w_v, [], [], V_COLS, "proj_v", tm=tm_proj)
        gates = _proj_call(functools.partial(_proj_gate_body, chunk=2 * MXU_N),
                           h, w_gate, [gate_bias[l:l + 1]], [_resident((1, GATE_COLS))],
                           GATE_COLS, "proj_gate", tm=tm_proj)

        rope3 = rope.reshape(b, s, ROPE_COLS)
        v3 = vals.reshape(b, s, V_COLS)
        out_a = _attn_a(rope3, v3, sink_logit[l])
        lam_vecs = jnp.stack([lambda_q1[l], lambda_k1[l], lambda_q2[l], lambda_k2[l]]).astype(F32)
        out_b = _attn_b(rope3, v3, lam_vecs, row(subln_g), lam_init)

        xt = _merge(out_a.reshape(t, A_Q), out_b.reshape(t, B_V), gates, xt,
                    bf(w_proj_a), bf(w_proj_b), bf(w_out), row(mix_post_g))

        xt = _ffn(xt, row(ffn2_pre_g), bf(ffn2_w_gate), bf(ffn2_w_up), bf(ffn2_w_down),
                  row(ffn2_post_g))
    return xt.reshape(b, s, d)
```

```python
import functools
import math

import jax
import jax.numpy as jnp
from jax import lax
from jax.experimental import pallas as pl
from jax.experimental.pallas import tpu as pltpu

D_MODEL = 2048
HEAD_DIM = 128
A_Q_HEADS = 8
A_KV_HEADS = 2
A_REP = A_Q_HEADS // A_KV_HEADS
WINDOW = 128
B_HEADS = 4
D_FF = 5632
ROPE_THETA = 10000.0
EPS = 1e-6
NEG_INF = -1e30
LOG2E = 1.4426950408889634
QK_SCALE = HEAD_DIM ** -0.5 * LOG2E
FIXED_SHIFT_SPAN = 60.0
FIXED_GROUP = 30

A_Q = A_Q_HEADS * HEAD_DIM
A_KV = A_KV_HEADS * HEAD_DIM
B_QK = B_HEADS * 2 * HEAD_DIM
B_V = B_HEADS * 2 * HEAD_DIM
ROPE_COLS = A_Q + A_KV + 2 * B_QK
V_COLS = A_KV + B_V
GATE_COLS = 2 * D_MODEL

LANES = 128
MXU_N = 256
VMEM_LIMIT = 56 * 1024 * 1024

F32 = jnp.float32
BF16 = jnp.bfloat16


def _rms(x, g):
    ms = jnp.mean(x * x, axis=-1, keepdims=True)
    return x * lax.rsqrt(ms + EPS) * g


def _lane_tile(x, n):
    return jnp.concatenate([x] * n, axis=1)


def _resident(shape):
    return pl.BlockSpec(shape, lambda *_: (0,) * len(shape), pipeline_mode=pl.Buffered(1))


def _ffn_body(*refs, nf, tf, n_sub_ragged, emit_next):
    if emit_next:
        (x_ref, gpre_ref, wg_ref, wu_ref, wd_ref, gpost_ref, gnext_ref,
         o_ref, hn_ref, h_scr) = refs
    else:
        (x_ref, gpre_ref, wg_ref, wu_ref, wd_ref, gpost_ref, o_ref, h_scr) = refs
    f = pl.program_id(1)
    n_sub = wg_ref.shape[1] // tf

    def sub_blocks(n, h, first=False):
        for j in range(n):
            cols = slice(j * tf, (j + 1) * tf)
            gate = jnp.dot(h, wg_ref[:, cols], preferred_element_type=F32)
            up = jnp.dot(h, wu_ref[:, cols], preferred_element_type=F32)
            a = gate * (1.0 / (1.0 + jnp.exp(-gate))) * up
            y = jnp.dot(a.astype(BF16), wd_ref[cols, :], preferred_element_type=F32)
            if first and j == 0:
                o_ref[...] = y
            else:
                o_ref[...] += y

    @pl.when(f == 0)
    def _():
        h = _rms(x_ref[...], gpre_ref[...]).astype(BF16)
        h_scr[...] = h
        sub_blocks(n_sub_ragged, h, first=True)

    @pl.when((f > 0) & (f < nf - 1))
    def _():
        sub_blocks(n_sub, h_scr[...])

    @pl.when(f == nf - 1)
    def _():
        sub_blocks(n_sub, h_scr[...])
        xo = x_ref[...] + 0.5 * _rms(o_ref[...], gpost_ref[...])
        o_ref[...] = xo
        if emit_next:
            hn_ref[...] = _rms(xo, gnext_ref[...]).astype(BF16)


def _ffn(x, g_pre, wg, wu, wd, g_post, g_next=None, *, tm=512, tf=512, n_sub=2):
    t, d = x.shape
    blk = tf * n_sub
    nf = pl.cdiv(D_FF, blk)
    n_sub_ragged = (D_FF - (nf - 1) * blk) // tf
    assert (nf - 1) * blk + n_sub_ragged * tf == D_FF
    emit_next = g_next is not None
    row = pl.BlockSpec((tm, d), lambda i, f: (i, 0))
    vec = pl.BlockSpec((1, d), lambda i, f: (0, 0))
    wblk = lambda f: (f + nf - 1) % nf
    in_specs = [row, vec,
                pl.BlockSpec((d, blk), lambda i, f: (0, wblk(f))),
                pl.BlockSpec((d, blk), lambda i, f: (0, wblk(f))),
                pl.BlockSpec((blk, d), lambda i, f: (wblk(f), 0)),
                vec]
    args = [x, g_pre, wg, wu, wd, g_post]
    out_shape = [jax.ShapeDtypeStruct((t, d), F32)]
    out_specs = [row]
    if emit_next:
        in_specs.append(vec)
        args.append(g_next)
        out_shape.append(jax.ShapeDtypeStruct((t, d), BF16))
        out_specs.append(row)
    res = pl.pallas_call(
        functools.partial(_ffn_body, nf=nf, tf=tf, n_sub_ragged=n_sub_ragged, emit_next=emit_next),
        grid=(t // tm, nf),
        in_specs=in_specs,
        out_specs=out_specs,
        out_shape=out_shape,
        scratch_shapes=[pltpu.VMEM((tm, d), BF16)],
        compiler_params=pltpu.CompilerParams(
            dimension_semantics=("parallel", "arbitrary"), vmem_limit_bytes=VMEM_LIMIT),
        name="ffn_next" if emit_next else "ffn",
    )(*args)
    return res if emit_next else res[0]


def _proj_rope_body(h_ref, w_ref, cos_ref, sin_ref, o_ref, *, head_scales):
    h = h_ref[...]
    cos = cos_ref[...]
    sin = sin_ref[...]
    heads_per_chunk = MXU_N // HEAD_DIM
    for c in range(len(head_scales) // heads_per_chunk):
        acc = jnp.dot(h, w_ref[:, c * MXU_N:(c + 1) * MXU_N], preferred_element_type=F32)
        for k in range(heads_per_chunk):
            hd = c * heads_per_chunk + k
            a = acc[:, k * HEAD_DIM:(k + 1) * HEAD_DIM]
            r = a * cos + pltpu.roll(a, HEAD_DIM // 2, 1) * sin
            if head_scales[hd] != 1.0:
                r = r * head_scales[hd]
            o_ref[:, hd * HEAD_DIM:(hd + 1) * HEAD_DIM] = r.astype(BF16)


def _proj_plain_body(h_ref, w_ref, o_ref, *, chunk):
    h = h_ref[...]
    for c in range(w_ref.shape[1] // chunk):
        acc = jnp.dot(h, w_ref[:, c * chunk:(c + 1) * chunk], preferred_element_type=F32)
        o_ref[:, c * chunk:(c + 1) * chunk] = acc.astype(BF16)


def _proj_gate_body(h_ref, w_ref, b_ref, o_ref, *, chunk):
    h = h_ref[...]
    for c in range(w_ref.shape[1] // chunk):
        acc = jnp.dot(h, w_ref[:, c * chunk:(c + 1) * chunk], preferred_element_type=F32)
        z = acc + b_ref[:, c * chunk:(c + 1) * chunk]
        o_ref[:, c * chunk:(c + 1) * chunk] = (1.0 / (1.0 + jnp.exp(-z))).astype(BF16)


def _proj_call(body, h, w, extra, extra_specs, n_out, name, *, tm):
    t, d = h.shape
    return pl.pallas_call(
        body,
        grid=(t // tm,),
        in_specs=[pl.BlockSpec((tm, d), lambda i: (i, 0)), _resident(w.shape)] + extra_specs,
        out_specs=pl.BlockSpec((tm, n_out), lambda i: (i, 0)),
        out_shape=jax.ShapeDtypeStruct((t, n_out), BF16),
        compiler_params=pltpu.CompilerParams(
            dimension_semantics=("parallel",), vmem_limit_bytes=VMEM_LIMIT),
        name=name,
    )(h, w, *extra)


def _attn_a_body(sink_ref, q_ref, kp_ref, kc_ref, kn_ref, vp_ref, vc_ref, vn_ref, o_ref, *, tq, nq):
    i = pl.program_id(1)
    nk = tq + 2 * WINDOW
    row = lax.broadcasted_iota(jnp.int32, (tq, nk), 0)
    col = lax.broadcasted_iota(jnp.int32, (tq, nk), 1)
    rel = col - WINDOW - row
    valid = (rel >= -WINDOW) & (rel <= WINDOW)
    valid = valid & ((col >= WINDOW) | (i > 0))
    valid = valid & ((col < tq + WINDOW) | (i < nq - 1))
    k_all = jnp.concatenate([kp_ref[0], kc_ref[0], kn_ref[0]], axis=0)
    v_all = jnp.concatenate([vp_ref[0], vc_ref[0], vn_ref[0]], axis=0)
    for g in range(A_KV_HEADS):
        k = k_all[:, g * HEAD_DIM:(g + 1) * HEAD_DIM]
        v = v_all[:, g * HEAD_DIM:(g + 1) * HEAD_DIM]
        for r in range(A_REP):
            hd = g * A_REP + r
            q = q_ref[0, :, hd * HEAD_DIM:(hd + 1) * HEAD_DIM]
            s = lax.dot_general(q, k, (((1,), (1,)), ((), ())), preferred_element_type=F32)
            s = jnp.where(valid, s, NEG_INF)
            sk = sink_ref[hd] * LOG2E
            m = jnp.maximum(jnp.max(s, axis=-1, keepdims=True), sk)
            e = jnp.exp2(s - m)
            den = jnp.sum(e, axis=-1, keepdims=True) + jnp.exp2(sk - m)
            o = jnp.dot(e.astype(BF16), v, preferred_element_type=F32) / den
            o_ref[0, :, hd * HEAD_DIM:(hd + 1) * HEAD_DIM] = o.astype(BF16)


def _attn_a(rope3, v3, sink, *, tq=512):
    b, s, _ = rope3.shape
    nq = s // tq
    nblk = s // WINDOW
    per = tq // WINDOW
    ka_blk = A_Q // A_KV
    prev_map = lambda bi, i: (bi, jnp.maximum(i * per - 1, 0), ka_blk)
    cur_map = lambda bi, i: (bi, i, ka_blk)
    next_map = lambda bi, i: (bi, jnp.minimum((i + 1) * per, nblk - 1), ka_blk)
    vprev_map = lambda bi, i: (bi, jnp.maximum(i * per - 1, 0), 0)
    vcur_map = lambda bi, i: (bi, i, 0)
    vnext_map = lambda bi, i: (bi, jnp.minimum((i + 1) * per, nblk - 1), 0)
    return pl.pallas_call(
        functools.partial(_attn_a_body, tq=tq, nq=nq),
        grid=(b, nq),
        in_specs=[
            pl.BlockSpec(memory_space=pltpu.SMEM),
            pl.BlockSpec((1, tq, A_Q), lambda bi, i: (bi, i, 0)),
            pl.BlockSpec((1, WINDOW, A_KV), prev_map),
            pl.BlockSpec((1, tq, A_KV), cur_map),
            pl.BlockSpec((1, WINDOW, A_KV), next_map),
            pl.BlockSpec((1, WINDOW, A_KV), vprev_map),
            pl.BlockSpec((1, tq, A_KV), vcur_map),
            pl.BlockSpec((1, WINDOW, A_KV), vnext_map),
        ],
        out_specs=pl.BlockSpec((1, tq, A_Q), lambda bi, i: (bi, i, 0)),
        out_shape=jax.ShapeDtypeStruct((b, s, A_Q), BF16),
        compiler_params=pltpu.CompilerParams(
            dimension_semantics=("parallel", "parallel"), vmem_limit_bytes=VMEM_LIMIT),
        name="attn_a",
    )(sink, rope3, rope3, rope3, rope3, v3, v3, v3)


def _attn_b_body(q_ref, k_ref, v_ref, lam_ref, g_ref, o_ref,
                 s_scr, mx_scr, p_scr, a_scr, m_scr, l_scr, acc_scr, kn_scr, *, tk, nk, lam_init):
    reps = tk // LANES
    tq = q_ref.shape[1]

    def key_chunk(c):
        return k_ref[0, pl.ds(pl.multiple_of(c * tk, tk), tk), :]

    def val_chunk(c):
        return v_ref[0, pl.ds(pl.multiple_of(c * tk, tk), tk), :]

    def qk(c, comp):
        q = q_ref[0, :, comp * HEAD_DIM:(comp + 1) * HEAD_DIM]
        k = key_chunk(c)[:, comp * HEAD_DIM:(comp + 1) * HEAD_DIM]
        return lax.dot_general(q, k, (((1,), (1,)), ((), ())), preferred_element_type=F32)

    def lane_fold(x, op):
        y = x[:, :LANES]
        for j in range(1, reps):
            y = op(y, x[:, j * LANES:(j + 1) * LANES])
        return y

    @pl.when(pl.program_id(2) == 0)
    def _():
        def chunk_max(c, carry):
            kc = key_chunk(c).astype(F32)
            sq = kc * kc
            out = []
            for comp in range(2):
                n2 = jnp.sum(sq[:, comp * HEAD_DIM:(comp + 1) * HEAD_DIM], axis=-1, keepdims=True)
                out.append(jnp.maximum(carry[comp], jnp.max(n2, axis=0, keepdims=True)))
            return tuple(out)
        zero = jnp.zeros((1, 1), F32)
        n2max = lax.fori_loop(0, nk, chunk_max, (zero, zero))
        for comp in range(2):
            kn_scr[comp] = jnp.broadcast_to(jnp.sqrt(n2max[comp]), kn_scr.shape[1:])

    def scores(c, slot):
        for comp in range(2):
            s = qk(c, comp)
            s_scr[slot, comp] = s
            mx_scr[slot, comp] = lane_fold(s, jnp.maximum)

    def softmax(slot):
        for comp in range(2):
            m_prev = m_scr[comp]
            m_new = jnp.maximum(m_prev, jnp.max(mx_scr[slot, comp], axis=-1, keepdims=True))
            a_scr[slot, comp] = jnp.exp2(m_prev - m_new)
            m_scr[comp] = m_new
            e = jnp.exp2(s_scr[slot, comp] - _lane_tile(m_new, reps))
            l_scr[comp] = a_scr[slot, comp] * l_scr[comp] + lane_fold(e, jnp.add)
            p_scr[slot, comp] = e.astype(BF16)

    def pv(c, slot):
        vc = val_chunk(c)
        for comp in range(2):
            acc_scr[comp] = _lane_tile(a_scr[slot, comp], 2) * acc_scr[comp] + jnp.dot(
                p_scr[slot, comp], vc, preferred_element_type=F32)

    def steady_pair(i):
        pv(i - 1, 0)
        scores(i + 1, 0)
        softmax(1)
        pv(i, 1)
        scores(i + 2, 1)
        softmax(0)

    def steady_quad(j, carry):
        steady_pair(4 * j + 1)
        steady_pair(4 * j + 3)
        return carry

    def probs(c, slot, s_of):
        for comp in range(2):
            e = jnp.exp2(s_of(comp) - _lane_tile(m_scr[comp], reps))
            l_scr[comp] += lane_fold(e, jnp.add)
            p_scr[slot, comp] = e.astype(BF16)

    def pv_plain(c, slot):
        vc = val_chunk(c)
        for comp in range(2):
            acc_scr[comp] += jnp.dot(p_scr[slot, comp], vc, preferred_element_type=F32)

    def fixed_group(t, carry):
        for r in range(FIXED_GROUP):
            c = 2 + FIXED_GROUP * t + r
            pv_plain(c - 2, r % 2)
            probs(c, r % 2, lambda comp: qk(c, comp))
        return carry

    l_scr[...] = jnp.zeros_like(l_scr)
    acc_scr[...] = jnp.zeros_like(acc_scr)
    scores(0, 0)
    scores(1, 1)
    span = jnp.zeros((1, 1), F32)
    for comp in range(2):
        qf = q_ref[0, :, comp * HEAD_DIM:(comp + 1) * HEAD_DIM].astype(F32)
        q_norm = jnp.sqrt(jnp.sum(qf * qf, axis=-1, keepdims=True))
        bound = q_norm * kn_scr[comp][:1, :1]
        first_max = jnp.max(mx_scr[0, comp], axis=-1, keepdims=True)
        span = jnp.maximum(span, jnp.max(bound - first_max, axis=0, keepdims=True))
        m_scr[comp] = jnp.broadcast_to(first_max, (tq, LANES))
    use_fixed = span[0, 0] <= FIXED_SHIFT_SPAN
    probs(0, 0, lambda comp: s_scr[0, comp])

    @pl.when(use_fixed)
    def _():
        probs(1, 1, lambda comp: s_scr[1, comp])
        lax.fori_loop(0, (nk - 2) // FIXED_GROUP, fixed_group, 0)
        pv_plain(nk - 2, 0)
        pv_plain(nk - 1, 1)

    @pl.when(jnp.logical_not(use_fixed))
    def _():
        m_scr[...] = jnp.full_like(m_scr, NEG_INF)
        l_scr[...] = jnp.zeros_like(l_scr)
        softmax(0)
        n_pairs = (nk - 2) // 2
        lax.fori_loop(0, n_pairs // 2, steady_quad, 0)
        if n_pairs % 2:
            steady_pair(nk - 3)
        softmax(1)
        pv(nk - 2, 0)
        pv(nk - 1, 1)

    lam_v = lam_ref[...]
    lam = (jnp.exp(jnp.sum(lam_v[0:1] * lam_v[1:2], axis=-1, keepdims=True))
           - jnp.exp(jnp.sum(lam_v[2:3] * lam_v[3:4], axis=-1, keepdims=True))
           + lam_init)
    l0 = jnp.sum(l_scr[0], axis=-1, keepdims=True)
    l1 = jnp.sum(l_scr[1], axis=-1, keepdims=True)
    o = acc_scr[0] / l0 - lam * (acc_scr[1] / l1)
    o_ref[0] = (_rms(o, g_ref[...]) * (1.0 - lam_init)).astype(BF16)


def _attn_b(rope3, v3, lam_vecs, subln_g, lam_init, *, tq=512, tk=512):
    b, s, _ = rope3.shape
    nk = s // tk
    assert nk % 2 == 0 and nk >= 4 and (nk - 2) % FIXED_GROUP == 0 and FIXED_GROUP % 2 == 0
    w = 2 * HEAD_DIM
    qb_blk = (A_Q + A_KV) // w
    kb_blk = qb_blk + B_HEADS
    vb_blk = A_KV // w
    return pl.pallas_call(
        functools.partial(_attn_b_body, tk=tk, nk=nk, lam_init=lam_init),
        grid=(b, B_HEADS, s // tq),
        in_specs=[
            pl.BlockSpec((1, tq, w), lambda bi, h, i: (bi, i, qb_blk + h)),
            pl.BlockSpec((1, s, w), lambda bi, h, i: (bi, 0, kb_blk + h)),
            pl.BlockSpec((1, s, w), lambda bi, h, i: (bi, 0, vb_blk + h)),
            pl.BlockSpec((4, HEAD_DIM), lambda bi, h, i: (0, 0)),
            pl.BlockSpec((1, w), lambda bi, h, i: (0, 0)),
        ],
        out_specs=pl.BlockSpec((1, tq, w), lambda bi, h, i: (bi, i, h)),
        out_shape=jax.ShapeDtypeStruct((b, s, B_V), BF16),
        scratch_shapes=[pltpu.VMEM((2, 2, tq, tk), F32),
                        pltpu.VMEM((2, 2, tq, LANES), F32),
                        pltpu.VMEM((2, 2, tq, tk), BF16),
                        pltpu.VMEM((2, 2, tq, LANES), F32),
                        pltpu.VMEM((2, tq, LANES), F32),
                        pltpu.VMEM((2, tq, LANES), F32),
                        pltpu.VMEM((2, tq, w), F32),
                        pltpu.VMEM((2, 8, LANES), F32)],
        compiler_params=pltpu.CompilerParams(
            dimension_semantics=("parallel", "parallel", "arbitrary"),
            vmem_limit_bytes=VMEM_LIMIT),
        name="attn_b",
    )(rope3, rope3, v3, lam_vecs, subln_g)


def _merge_body(oa_ref, ob_ref, gate_ref, x_ref, wa_ref, wb_ref, wo_ref, g_ref, o_ref):
    pa = jnp.dot(oa_ref[...], wa_ref[...], preferred_element_type=F32)
    pb = jnp.dot(ob_ref[...], wb_ref[...], preferred_element_type=F32)
    ga = gate_ref[:, :D_MODEL].astype(F32)
    gb = gate_ref[:, D_MODEL:].astype(F32)
    merged = (ga * pa + gb * pb).astype(BF16)
    y = jnp.dot(merged, wo_ref[...], preferred_element_type=F32)
    o_ref[...] = x_ref[...] + _rms(y, g_ref[...])


def _merge(oa, ob, gates, x1, wa, wb, wo, g_post, *, tm=512):
    t, d = x1.shape
    rows = lambda n: pl.BlockSpec((tm, n), lambda i: (i, 0))
    return pl.pallas_call(
        _merge_body,
        grid=(t // tm,),
        in_specs=[rows(A_Q), rows(B_V), rows(GATE_COLS), rows(d),
                  _resident(wa.shape), _resident(wb.shape), _resident(wo.shape),
                  pl.BlockSpec((1, d), lambda i: (0, 0))],
        out_specs=rows(d),
        out_shape=jax.ShapeDtypeStruct((t, d), F32),
        compiler_params=pltpu.CompilerParams(
            dimension_semantics=("parallel",), vmem_limit_bytes=VMEM_LIMIT),
        name="merge",
    )(oa, ob, gates, x1, wa, wb, wo, g_post)


def _rope_tables(seq):
    pos = jnp.arange(seq, dtype=F32)
    inv = ROPE_THETA ** (-jnp.arange(0, HEAD_DIM, 2, dtype=F32) / HEAD_DIM)
    ang = pos[:, None] * inv[None, :]
    cos, sin = jnp.cos(ang), jnp.sin(ang)
    return jnp.concatenate([cos, cos], axis=-1), jnp.concatenate([-sin, sin], axis=-1)


def kernel(x, ffn1_pre_g, ffn1_w_gate, ffn1_w_up, ffn1_w_down, ffn1_post_g, mix_pre_g, w_in, gate_bias, sink_logit, lambda_q1, lambda_k1, lambda_q2, lambda_k2, subln_g, w_proj_a, w_proj_b, w_out, mix_post_g, ffn2_pre_g, ffn2_w_gate, ffn2_w_up, ffn2_w_down, ffn2_post_g):
    b, s, d = x.shape
    t = b * s
    depth = ffn1_pre_g.shape[0]
    cos2, sin2 = _rope_tables(s)
    xt = x.reshape(t, d)
    q_scaled = (QK_SCALE,) * A_Q_HEADS + (1.0,) * A_KV_HEADS + (QK_SCALE,) * (2 * B_HEADS) + (1.0,) * (2 * B_HEADS)
    tm_proj = 1024
    for l in range(depth):
        lam_init = 0.8 - 0.6 * math.exp(-0.3 * l)
        row = lambda p: p[l:l + 1]
        bf = lambda p: p[l].astype(BF16)

        xt, h = _ffn(xt, row(ffn1_pre_g), bf(ffn1_w_gate), bf(ffn1_w_up), bf(ffn1_w_down),
                     row(ffn1_post_g), row(mix_pre_g))

        w = w_in[l]
        qa_end, ka_end, va_end = A_Q, A_Q + A_KV, A_Q + 2 * A_KV
        qb_end = va_end + B_QK
        kb_end = qb_end + B_QK
        vb_end = kb_end + B_V
        w_rope = jnp.concatenate([w[:, :ka_end], w[:, va_end:kb_end]], axis=1).astype(BF16)
        w_v = jnp.concatenate([w[:, ka_end:va_end], w[:, kb_end:vb_end]], axis=1).astype(BF16)
        w_gate = w[:, vb_end:].astype(BF16)

        ns = s // tm_proj
        tab = pl.BlockSpec((tm_proj, HEAD_DIM), lambda i: (i % ns, 0))
        rope = _proj_call(functools.partial(_proj_rope_body, head_scales=q_scaled),
                          h, w_rope, [cos2, sin2], [tab, tab], ROPE_COLS, "proj_rope", tm=tm_proj)
        vals = _proj_call(functools.partial(_proj_plain_body, chunk=MXU_N),
                          h, w_v, [], [], V_COLS, "proj_v", tm=tm_proj)
        gates = _proj_call(functools.partial(_proj_gate_body, chunk=2 * MXU_N),
                           h, w_gate, [gate_bias[l:l + 1]], [_resident((1, GATE_COLS))],
                           GATE_COLS, "proj_gate", tm=tm_proj)

        rope3 = rope.reshape(b, s, ROPE_COLS)
        v3 = vals.reshape(b, s, V_COLS)
        out_a = _attn_a(rope3, v3, sink_logit[l])
        lam_vecs = jnp.stack([lambda_q1[l], lambda_k1[l], lambda_q2[l], lambda_k2[l]]).astype(F32)
        out_b = _attn_b(rope3, v3, lam_vecs, row(subln_g), lam_init)

        xt = _merge(out_a.reshape(t, A_Q), out_b.reshape(t, B_V), gates, xt,
                    bf(w_proj_a), bf(w_proj_b), bf(w_out), row(mix_post_g))

        xt = _ffn(xt, row(ffn2_pre_g), bf(ffn2_w_gate), bf(ffn2_w_up), bf(ffn2_w_down),
                  row(ffn2_post_g))
    return xt.reshape(b, s, d)
```

```python
import functools
import math

import jax
import jax.numpy as jnp
from jax import lax
from jax.experimental import pallas as pl
from jax.experimental.pallas import tpu as pltpu

D_MODEL = 2048
HEAD_DIM = 128
A_Q_HEADS = 8
A_KV_HEADS = 2
A_REP = A_Q_HEADS // A_KV_HEADS
WINDOW = 128
B_HEADS = 4
D_FF = 5632
ROPE_THETA = 10000.0
EPS = 1e-6
NEG_INF = -1e30
LOG2E = 1.4426950408889634
QK_SCALE = HEAD_DIM ** -0.5 * LOG2E
FIXED_SHIFT_SPAN = 60.0
FIXED_GROUP = 30

A_Q = A_Q_HEADS * HEAD_DIM
A_KV = A_KV_HEADS * HEAD_DIM
B_QK = B_HEADS * 2 * HEAD_DIM
B_V = B_HEADS * 2 * HEAD_DIM
ROPE_COLS = A_Q + A_KV + 2 * B_QK
V_COLS = A_KV + B_V
GATE_COLS = 2 * D_MODEL

LANES = 128
MXU_N = 256
VMEM_LIMIT = 56 * 1024 * 1024

F32 = jnp.float32
BF16 = jnp.bfloat16


def _rms(x, g):
    ms = jnp.mean(x * x, axis=-1, keepdims=True)
    return x * lax.rsqrt(ms + EPS) * g


def _lane_tile(x, n):
    return jnp.concatenate([x] * n, axis=1)


def _resident(shape):
    return pl.BlockSpec(shape, lambda *_: (0,) * len(shape), pipeline_mode=pl.Buffered(1))


def _ffn_body(*refs, nf, tf, n_sub_last, emit_next):
    if emit_next:
        (x_ref, gpre_ref, wg_ref, wu_ref, wd_ref, gpost_ref, gnext_ref,
         o_ref, hn_ref, h_scr) = refs
    else:
        (x_ref, gpre_ref, wg_ref, wu_ref, wd_ref, gpost_ref, o_ref, h_scr) = refs
    f = pl.program_id(1)
    n_sub = wg_ref.shape[2] // tf

    def sub_blocks(n, h, first=False):
        for j in range(n):
            cols = slice(j * tf, (j + 1) * tf)
            gate = jnp.dot(h, wg_ref[0, :, cols], preferred_element_type=F32)
            up = jnp.dot(h, wu_ref[0, :, cols], preferred_element_type=F32)
            a = gate * (1.0 / (1.0 + jnp.exp(-gate))) * up
            y = jnp.dot(a.astype(BF16), wd_ref[cols, :], preferred_element_type=F32)
            if first and j == 0:
                o_ref[...] = y
            else:
                o_ref[...] += y

    @pl.when(f == 0)
    def _():
        h = _rms(x_ref[...], gpre_ref[...]).astype(BF16)
        h_scr[...] = h
        sub_blocks(n_sub, h, first=True)

    @pl.when((f > 0) & (f < nf - 1))
    def _():
        sub_blocks(n_sub, h_scr[...])

    @pl.when(f == nf - 1)
    def _():
        sub_blocks(n_sub_last, h_scr[...])
        xo = x_ref[...] + 0.5 * _rms(o_ref[...], gpost_ref[...])
        o_ref[...] = xo
        if emit_next:
            hn_ref[...] = _rms(xo, gnext_ref[...]).astype(BF16)


def _col_blocks(w, blk):
    d, n = w.shape
    nb = pl.cdiv(n, blk)
    w = jnp.pad(w, ((0, 0), (0, nb * blk - n)))
    return w.reshape(d, nb, blk).transpose(1, 0, 2)


def _ffn(x, g_pre, wg, wu, wd, g_post, g_next=None, *, tm=512, tf=512, n_sub=2):
    t, d = x.shape
    blk = tf * n_sub
    nf = pl.cdiv(D_FF, blk)
    n_sub_last = (D_FF - (nf - 1) * blk) // tf
    assert (nf - 1) * blk + n_sub_last * tf == D_FF
    emit_next = g_next is not None
    row = pl.BlockSpec((tm, d), lambda i, f: (i, 0))
    vec = pl.BlockSpec((1, d), lambda i, f: (0, 0))
    in_specs = [row, vec,
                pl.BlockSpec((1, d, blk), lambda i, f: (f, 0, 0)),
                pl.BlockSpec((1, d, blk), lambda i, f: (f, 0, 0)),
                pl.BlockSpec((blk, d), lambda i, f: (f, 0)),
                vec]
    args = [x, g_pre, _col_blocks(wg, blk), _col_blocks(wu, blk), wd, g_post]
    out_shape = [jax.ShapeDtypeStruct((t, d), F32)]
    out_specs = [row]
    if emit_next:
        in_specs.append(vec)
        args.append(g_next)
        out_shape.append(jax.ShapeDtypeStruct((t, d), BF16))
        out_specs.append(row)
    res = pl.pallas_call(
        functools.partial(_ffn_body, nf=nf, tf=tf, n_sub_last=n_sub_last, emit_next=emit_next),
        grid=(t // tm, nf),
        in_specs=in_specs,
        out_specs=out_specs,
        out_shape=out_shape,
        scratch_shapes=[pltpu.VMEM((tm, d), BF16)],
        compiler_params=pltpu.CompilerParams(
            dimension_semantics=("parallel", "arbitrary"), vmem_limit_bytes=VMEM_LIMIT),
        name="ffn_next" if emit_next else "ffn",
    )(*args)
    return res if emit_next else res[0]


def _proj_rope_body(h_ref, w_ref, cos_ref, sin_ref, o_ref, *, head_scales):
    h = h_ref[...]
    cos = cos_ref[...]
    sin = sin_ref[...]
    heads_per_chunk = MXU_N // HEAD_DIM
    for c in range(len(head_scales) // heads_per_chunk):
        acc = jnp.dot(h, w_ref[:, c * MXU_N:(c + 1) * MXU_N], preferred_element_type=F32)
        for k in range(heads_per_chunk):
            hd = c * heads_per_chunk + k
            a = acc[:, k * HEAD_DIM:(k + 1) * HEAD_DIM]
            r = a * cos + pltpu.roll(a, HEAD_DIM // 2, 1) * sin
            if head_scales[hd] != 1.0:
                r = r * head_scales[hd]
            o_ref[:, hd * HEAD_DIM:(hd + 1) * HEAD_DIM] = r.astype(BF16)


def _proj_plain_body(h_ref, w_ref, o_ref, *, chunk):
    h = h_ref[...]
    for c in range(w_ref.shape[1] // chunk):
        acc = jnp.dot(h, w_ref[:, c * chunk:(c + 1) * chunk], preferred_element_type=F32)
        o_ref[:, c * chunk:(c + 1) * chunk] = acc.astype(BF16)


def _proj_gate_body(h_ref, w_ref, b_ref, o_ref, *, chunk):
    h = h_ref[...]
    for c in range(w_ref.shape[1] // chunk):
        acc = jnp.dot(h, w_ref[:, c * chunk:(c + 1) * chunk], preferred_element_type=F32)
        z = acc + b_ref[:, c * chunk:(c + 1) * chunk]
        o_ref[:, c * chunk:(c + 1) * chunk] = (1.0 / (1.0 + jnp.exp(-z))).astype(BF16)


def _proj_call(body, h, w, extra, extra_specs, n_out, name, *, tm):
    t, d = h.shape
    return pl.pallas_call(
        body,
        grid=(t // tm,),
        in_specs=[pl.BlockSpec((tm, d), lambda i: (i, 0)), _resident(w.shape)] + extra_specs,
        out_specs=pl.BlockSpec((tm, n_out), lambda i: (i, 0)),
        out_shape=jax.ShapeDtypeStruct((t, n_out), BF16),
        compiler_params=pltpu.CompilerParams(
            dimension_semantics=("parallel",), vmem_limit_bytes=VMEM_LIMIT),
        name=name,
    )(h, w, *extra)


def _attn_a_body(sink_ref, q_ref, kp_ref, kc_ref, kn_ref, vp_ref, vc_ref, vn_ref, o_ref, *, tq, nq):
    i = pl.program_id(1)
    nk = tq + 2 * WINDOW
    row = lax.broadcasted_iota(jnp.int32, (tq, nk), 0)
    col = lax.broadcasted_iota(jnp.int32, (tq, nk), 1)
    rel = col - WINDOW - row
    valid = (rel >= -WINDOW) & (rel <= WINDOW)
    valid = valid & ((col >= WINDOW) | (i > 0))
    valid = valid & ((col < tq + WINDOW) | (i < nq - 1))
    k_all = jnp.concatenate([kp_ref[0], kc_ref[0], kn_ref[0]], axis=0)
    v_all = jnp.concatenate([vp_ref[0], vc_ref[0], vn_ref[0]], axis=0)
    for g in range(A_KV_HEADS):
        k = k_all[:, g * HEAD_DIM:(g + 1) * HEAD_DIM]
        v = v_all[:, g * HEAD_DIM:(g + 1) * HEAD_DIM]
        for r in range(A_REP):
            hd = g * A_REP + r
            q = q_ref[0, :, hd * HEAD_DIM:(hd + 1) * HEAD_DIM]
            s = lax.dot_general(q, k, (((1,), (1,)), ((), ())), preferred_element_type=F32)
            s = jnp.where(valid, s, NEG_INF)
            sk = sink_ref[hd] * LOG2E
            m = jnp.maximum(jnp.max(s, axis=-1, keepdims=True), sk)
            e = jnp.exp2(s - m)
            den = jnp.sum(e, axis=-1, keepdims=True) + jnp.exp2(sk - m)
            o = jnp.dot(e.astype(BF16), v, preferred_element_type=F32) / den
            o_ref[0, :, hd * HEAD_DIM:(hd + 1) * HEAD_DIM] = o.astype(BF16)


def _attn_a(rope3, v3, sink, *, tq=512):
    b, s, _ = rope3.shape
    nq = s // tq
    nblk = s // WINDOW
    per = tq // WINDOW
    ka_blk = A_Q // A_KV
    prev_map = lambda bi, i: (bi, jnp.maximum(i * per - 1, 0), ka_blk)
    cur_map = lambda bi, i: (bi, i, ka_blk)
    next_map = lambda bi, i: (bi, jnp.minimum((i + 1) * per, nblk - 1), ka_blk)
    vprev_map = lambda bi, i: (bi, jnp.maximum(i * per - 1, 0), 0)
    vcur_map = lambda bi, i: (bi, i, 0)
    vnext_map = lambda bi, i: (bi, jnp.minimum((i + 1) * per, nblk - 1), 0)
    return pl.pallas_call(
        functools.partial(_attn_a_body, tq=tq, nq=nq),
        grid=(b, nq),
        in_specs=[
            pl.BlockSpec(memory_space=pltpu.SMEM),
            pl.BlockSpec((1, tq, A_Q), lambda bi, i: (bi, i, 0)),
            pl.BlockSpec((1, WINDOW, A_KV), prev_map),
            pl.BlockSpec((1, tq, A_KV), cur_map),
            pl.BlockSpec((1, WINDOW, A_KV), next_map),
            pl.BlockSpec((1, WINDOW, A_KV), vprev_map),
            pl.BlockSpec((1, tq, A_KV), vcur_map),
            pl.BlockSpec((1, WINDOW, A_KV), vnext_map),
        ],
        out_specs=pl.BlockSpec((1, tq, A_Q), lambda bi, i: (bi, i, 0)),
        out_shape=jax.ShapeDtypeStruct((b, s, A_Q), BF16),
        compiler_params=pltpu.CompilerParams(
            dimension_semantics=("parallel", "parallel"), vmem_limit_bytes=VMEM_LIMIT),
        name="attn_a",
    )(sink, rope3, rope3, rope3, rope3, v3, v3, v3)


def _attn_b_body(q_ref, k_ref, v_ref, lam_ref, g_ref, o_ref,
                 s_scr, mx_scr, p_scr, a_scr, m_scr, l_scr, acc_scr, kn_scr, *, tk, nk, lam_init):
    reps = tk // LANES
    tq = q_ref.shape[1]

    def key_chunk(c):
        return k_ref[0, pl.ds(pl.multiple_of(c * tk, tk), tk), :]

    def val_chunk(c):
        return v_ref[0, pl.ds(pl.multiple_of(c * tk, tk), tk), :]

    def qk(c, comp):
        q = q_ref[0, :, comp * HEAD_DIM:(comp + 1) * HEAD_DIM]
        k = key_chunk(c)[:, comp * HEAD_DIM:(comp + 1) * HEAD_DIM]
        return lax.dot_general(q, k, (((1,), (1,)), ((), ())), preferred_element_type=F32)

    def lane_fold(x, op):
        y = x[:, :LANES]
        for j in range(1, reps):
            y = op(y, x[:, j * LANES:(j + 1) * LANES])
        return y

    @pl.when(pl.program_id(2) == 0)
    def _():
        def chunk_max(c, carry):
            kc = key_chunk(c).astype(F32)
            sq = kc * kc
            out = []
            for comp in range(2):
                n2 = jnp.sum(sq[:, comp * HEAD_DIM:(comp + 1) * HEAD_DIM], axis=-1, keepdims=True)
                out.append(jnp.maximum(carry[comp], jnp.max(n2, axis=0, keepdims=True)))
            return tuple(out)
        zero = jnp.zeros((1, 1), F32)
        n2max = lax.fori_loop(0, nk, chunk_max, (zero, zero))
        for comp in range(2):
            kn_scr[comp] = jnp.broadcast_to(jnp.sqrt(n2max[comp]), kn_scr.shape[1:])

    def scores(c, slot):
        for comp in range(2):
            s = qk(c, comp)
            s_scr[slot, comp] = s
            mx_scr[slot, comp] = lane_fold(s, jnp.maximum)

    def softmax(slot):
        for comp in range(2):
            m_prev = m_scr[comp]
            m_new = jnp.maximum(m_prev, jnp.max(mx_scr[slot, comp], axis=-1, keepdims=True))
            a_scr[slot, comp] = jnp.exp2(m_prev - m_new)
            m_scr[comp] = m_new
            e = jnp.exp2(s_scr[slot, comp] - _lane_tile(m_new, reps))
            l_scr[comp] = a_scr[slot, comp] * l_scr[comp] + lane_fold(e, jnp.add)
            p_scr[slot, comp] = e.astype(BF16)

    def pv(c, slot):
        vc = val_chunk(c)
        for comp in range(2):
            acc_scr[comp] = _lane_tile(a_scr[slot, comp], 2) * acc_scr[comp] + jnp.dot(
                p_scr[slot, comp], vc, preferred_element_type=F32)

    def steady_pair(i):
        pv(i - 1, 0)
        scores(i + 1, 0)
        softmax(1)
        pv(i, 1)
        scores(i + 2, 1)
        softmax(0)

    def steady_quad(j, carry):
        steady_pair(4 * j + 1)
        steady_pair(4 * j + 3)
        return carry

    def probs(c, slot, s_of):
        for comp in range(2):
            e = jnp.exp2(s_of(comp) - _lane_tile(m_scr[comp], reps))
            l_scr[comp] += lane_fold(e, jnp.add)
            p_scr[slot, comp] = e.astype(BF16)

    def pv_plain(c, slot):
        vc = val_chunk(c)
        for comp in range(2):
            acc_scr[comp] += jnp.dot(p_scr[slot, comp], vc, preferred_element_type=F32)

    def fixed_group(t, carry):
        for r in range(FIXED_GROUP):
            c = 2 + FIXED_GROUP * t + r
            pv_plain(c - 2, r % 2)
            probs(c, r % 2, lambda comp: qk(c, comp))
        return carry

    l_scr[...] = jnp.zeros_like(l_scr)
    acc_scr[...] = jnp.zeros_like(acc_scr)
    scores(0, 0)
    scores(1, 1)
    span = jnp.zeros((1, 1), F32)
    for comp in range(2):
        qf = q_ref[0, :, comp * HEAD_DIM:(comp + 1) * HEAD_DIM].astype(F32)
        q_norm = jnp.sqrt(jnp.sum(qf * qf, axis=-1, keepdims=True))
        bound = q_norm * kn_scr[comp][:1, :1]
        first_max = jnp.max(mx_scr[0, comp], axis=-1, keepdims=True)
        span = jnp.maximum(span, jnp.max(bound - first_max, axis=0, keepdims=True))
        m_scr[comp] = jnp.broadcast_to(first_max, (tq, LANES))
    use_fixed = span[0, 0] <= FIXED_SHIFT_SPAN
    probs(0, 0, lambda comp: s_scr[0, comp])

    @pl.when(use_fixed)
    def _():
        probs(1, 1, lambda comp: s_scr[1, comp])
        lax.fori_loop(0, (nk - 2) // FIXED_GROUP, fixed_group, 0)
        pv_plain(nk - 2, 0)
        pv_plain(nk - 1, 1)

    @pl.when(jnp.logical_not(use_fixed))
    def _():
        m_scr[...] = jnp.full_like(m_scr, NEG_INF)
        l_scr[...] = jnp.zeros_like(l_scr)
        softmax(0)
        n_pairs = (nk - 2) // 2
        lax.fori_loop(0, n_pairs // 2, steady_quad, 0)
        if n_pairs % 2:
            steady_pair(nk - 3)
        softmax(1)
        pv(nk - 2, 0)
        pv(nk - 1, 1)

    lam_v = lam_ref[...]
    lam = (jnp.exp(jnp.sum(lam_v[0:1] * lam_v[1:2], axis=-1, keepdims=True))
           - jnp.exp(jnp.sum(lam_v[2:3] * lam_v[3:4], axis=-1, keepdims=True))
           + lam_init)
    l0 = jnp.sum(l_scr[0], axis=-1, keepdims=True)
    l1 = jnp.sum(l_scr[1], axis=-1, keepdims=True)
    o = acc_scr[0] / l0 - lam * (acc_scr[1] / l1)
    o_ref[0] = (_rms(o, g_ref[...]) * (1.0 - lam_init)).astype(BF16)


def _attn_b(rope3, v3, lam_vecs, subln_g, lam_init, *, tq=512, tk=512):
    b, s, _ = rope3.shape
    nk = s // tk
    assert nk % 2 == 0 and nk >= 4 and (nk - 2) % FIXED_GROUP == 0 and FIXED_GROUP % 2 == 0
    w = 2 * HEAD_DIM
    qb_blk = (A_Q + A_KV) // w
    kb_blk = qb_blk + B_HEADS
    vb_blk = A_KV // w
    return pl.pallas_call(
        functools.partial(_attn_b_body, tk=tk, nk=nk, lam_init=lam_init),
        grid=(b, B_HEADS, s // tq),
        in_specs=[
            pl.BlockSpec((1, tq, w), lambda bi, h, i: (bi, i, qb_blk + h)),
            pl.BlockSpec((1, s, w), lambda bi, h, i: (bi, 0, kb_blk + h)),
            pl.BlockSpec((1, s, w), lambda bi, h, i: (bi, 0, vb_blk + h)),
            pl.BlockSpec((4, HEAD_DIM), lambda bi, h, i: (0, 0)),
            pl.BlockSpec((1, w), lambda bi, h, i: (0, 0)),
        ],
        out_specs=pl.BlockSpec((1, tq, w), lambda bi, h, i: (bi, i, h)),
        out_shape=jax.ShapeDtypeStruct((b, s, B_V), BF16),
        scratch_shapes=[pltpu.VMEM((2, 2, tq, tk), F32),
                        pltpu.VMEM((2, 2, tq, LANES), F32),
                        pltpu.VMEM((2, 2, tq, tk), BF16),
                        pltpu.VMEM((2, 2, tq, LANES), F32),
                        pltpu.VMEM((2, tq, LANES), F32),
                        pltpu.VMEM((2, tq, LANES), F32),
                        pltpu.VMEM((2, tq, w), F32),
                        pltpu.VMEM((2, 8, LANES), F32)],
        compiler_params=pltpu.CompilerParams(
            dimension_semantics=("parallel", "parallel", "arbitrary"),
            vmem_limit_bytes=VMEM_LIMIT),
        name="attn_b",
    )(rope3, rope3, v3, lam_vecs, subln_g)


def _merge_body(oa_ref, ob_ref, gate_ref, x_ref, wa_ref, wb_ref, wo_ref, g_ref, o_ref):
    pa = jnp.dot(oa_ref[...], wa_ref[...], preferred_element_type=F32)
    pb = jnp.dot(ob_ref[...], wb_ref[...], preferred_element_type=F32)
    ga = gate_ref[:, :D_MODEL].astype(F32)
    gb = gate_ref[:, D_MODEL:].astype(F32)
    merged = (ga * pa + gb * pb).astype(BF16)
    y = jnp.dot(merged, wo_ref[...], preferred_element_type=F32)
    o_ref[...] = x_ref[...] + _rms(y, g_ref[...])


def _merge(oa, ob, gates, x1, wa, wb, wo, g_post, *, tm=512):
    t, d = x1.shape
    rows = lambda n: pl.BlockSpec((tm, n), lambda i: (i, 0))
    return pl.pallas_call(
        _merge_body,
        grid=(t // tm,),
        in_specs=[rows(A_Q), rows(B_V), rows(GATE_COLS), rows(d),
                  _resident(wa.shape), _resident(wb.shape), _resident(wo.shape),
                  pl.BlockSpec((1, d), lambda i: (0, 0))],
        out_specs=rows(d),
        out_shape=jax.ShapeDtypeStruct((t, d), F32),
        compiler_params=pltpu.CompilerParams(
            dimension_semantics=("parallel",), vmem_limit_bytes=VMEM_LIMIT),
        name="merge",
    )(oa, ob, gates, x1, wa, wb, wo, g_post)


def _rope_tables(seq):
    pos = jnp.arange(seq, dtype=F32)
    inv = ROPE_THETA ** (-jnp.arange(0, HEAD_DIM, 2, dtype=F32) / HEAD_DIM)
    ang = pos[:, None] * inv[None, :]
    cos, sin = jnp.cos(ang), jnp.sin(ang)
    return jnp.concatenate([cos, cos], axis=-1), jnp.concatenate([-sin, sin], axis=-1)


def kernel(x, ffn1_pre_g, ffn1_w_gate, ffn1_w_up, ffn1_w_down, ffn1_post_g, mix_pre_g, w_in, gate_bias, sink_logit, lambda_q1, lambda_k1, lambda_q2, lambda_k2, subln_g, w_proj_a, w_proj_b, w_out, mix_post_g, ffn2_pre_g, ffn2_w_gate, ffn2_w_up, ffn2_w_down, ffn2_post_g):
    b, s, d = x.shape
    t = b * s
    depth = ffn1_pre_g.shape[0]
    cos2, sin2 = _rope_tables(s)
    xt = x.reshape(t, d)
    q_scaled = (QK_SCALE,) * A_Q_HEADS + (1.0,) * A_KV_HEADS + (QK_SCALE,) * (2 * B_HEADS) + (1.0,) * (2 * B_HEADS)
    tm_proj = 1024
    for l in range(depth):
        lam_init = 0.8 - 0.6 * math.exp(-0.3 * l)
        row = lambda p: p[l:l + 1]
        bf = lambda p: p[l].astype(BF16)

        xt, h = _ffn(xt, row(ffn1_pre_g), bf(ffn1_w_gate), bf(ffn1_w_up), bf(ffn1_w_down),
                     row(ffn1_post_g), row(mix_pre_g))

        w = w_in[l]
        qa_end, ka_end, va_end = A_Q, A_Q + A_KV, A_Q + 2 * A_KV
        qb_end = va_end + B_QK
        kb_end = qb_end + B_QK
        vb_end = kb_end + B_V
        w_rope = jnp.concatenate([w[:, :ka_end], w[:, va_end:kb_end]], axis=1).astype(BF16)
        w_v = jnp.concatenate([w[:, ka_end:va_end], w[:, kb_end:vb_end]], axis=1).astype(BF16)
        w_gate = w[:, vb_end:].astype(BF16)

        ns = s // tm_proj
        tab = pl.BlockSpec((tm_proj, HEAD_DIM), lambda i: (i % ns, 0))
        rope = _proj_call(functools.partial(_proj_rope_body, head_scales=q_scaled),
                          h, w_rope, [cos2, sin2], [tab, tab], ROPE_COLS, "proj_rope", tm=tm_proj)
        vals = _proj_call(functools.partial(_proj_plain_body, chunk=MXU_N),
                          h, w_v, [], [], V_COLS, "proj_v", tm=tm_proj)
        gates = _proj_call(functools.partial(_proj_gate_body, chunk=2 * MXU_N),
                           h, w_gate, [gate_bias[l:l + 1]], [_resident((1, GATE_COLS))],
                           GATE_COLS, "proj_gate", tm=tm_proj)

        rope3 = rope.reshape(b, s, ROPE_COLS)
        v3 = vals.reshape(b, s, V_COLS)
        out_a = _attn_a(rope3, v3, sink_logit[l])
        lam_vecs = jnp.stack([lambda_q1[l], lambda_k1[l], lambda_q2[l], lambda_k2[l]]).astype(F32)
        out_b = _attn_b(rope3, v3, lam_vecs, row(subln_g), lam_init)

        xt = _merge(out_a.reshape(t, A_Q), out_b.reshape(t, B_V), gates, xt,
                    bf(w_proj_a), bf(w_proj_b), bf(w_out), row(mix_post_g))

        xt = _ffn(xt, row(ffn2_pre_g), bf(ffn2_w_gate), bf(ffn2_w_up), bf(ffn2_w_down),
                  row(ffn2_post_g))
    return xt.reshape(b, s, d)
```

```python
import functools
import math

import jax
import jax.numpy as jnp
from jax import lax
from jax.experimental import pallas as pl
from jax.experimental.pallas import tpu as pltpu

D_MODEL = 2048
HEAD_DIM = 128
A_Q_HEADS = 8
A_KV_HEADS = 2
A_REP = A_Q_HEADS // A_KV_HEADS
WINDOW = 128
B_HEADS = 4
D_FF = 5632
ROPE_THETA = 10000.0
EPS = 1e-6
NEG_INF = -1e30
LOG2E = 1.4426950408889634
QK_SCALE = HEAD_DIM ** -0.5 * LOG2E
FIXED_SHIFT_SPAN = 60.0

A_Q = A_Q_HEADS * HEAD_DIM
A_KV = A_KV_HEADS * HEAD_DIM
B_QK = B_HEADS * 2 * HEAD_DIM
B_V = B_HEADS * 2 * HEAD_DIM
ROPE_COLS = A_Q + A_KV + 2 * B_QK
V_COLS = A_KV + B_V
GATE_COLS = 2 * D_MODEL

LANES = 128
MXU_N = 256
VMEM_LIMIT = 56 * 1024 * 1024

F32 = jnp.float32
BF16 = jnp.bfloat16


def _rms(x, g):
    ms = jnp.mean(x * x, axis=-1, keepdims=True)
    return x * lax.rsqrt(ms + EPS) * g


def _lane_tile(x, n):
    return jnp.concatenate([x] * n, axis=1)


def _resident(shape):
    return pl.BlockSpec(shape, lambda *_: (0,) * len(shape), pipeline_mode=pl.Buffered(1))


def _ffn_body(*refs, nf, tf, n_sub_last, emit_next):
    if emit_next:
        (x_ref, gpre_ref, wg_ref, wu_ref, wd_ref, gpost_ref, gnext_ref,
         o_ref, hn_ref, h_scr) = refs
    else:
        (x_ref, gpre_ref, wg_ref, wu_ref, wd_ref, gpost_ref, o_ref, h_scr) = refs
    f = pl.program_id(1)
    n_sub = wg_ref.shape[1] // tf

    def sub_blocks(n, h, first=False):
        for j in range(n):
            cols = slice(j * tf, (j + 1) * tf)
            gate = jnp.dot(h, wg_ref[:, cols], preferred_element_type=F32)
            up = jnp.dot(h, wu_ref[:, cols], preferred_element_type=F32)
            a = gate * (1.0 / (1.0 + jnp.exp(-gate))) * up
            y = jnp.dot(a.astype(BF16), wd_ref[cols, :], preferred_element_type=F32)
            if first and j == 0:
                o_ref[...] = y
            else:
                o_ref[...] += y

    @pl.when(f == 0)
    def _():
        h = _rms(x_ref[...], gpre_ref[...]).astype(BF16)
        h_scr[...] = h
        sub_blocks(n_sub, h, first=True)

    @pl.when((f > 0) & (f < nf - 1))
    def _():
        sub_blocks(n_sub, h_scr[...])

    @pl.when(f == nf - 1)
    def _():
        sub_blocks(n_sub_last, h_scr[...])
        xo = x_ref[...] + 0.5 * _rms(o_ref[...], gpost_ref[...])
        o_ref[...] = xo
        if emit_next:
            hn_ref[...] = _rms(xo, gnext_ref[...]).astype(BF16)


def _ffn(x, g_pre, wg, wu, wd, g_post, g_next=None, *, tm=512, tf=512, n_sub=2):
    t, d = x.shape
    blk = tf * n_sub
    nf = pl.cdiv(D_FF, blk)
    n_sub_last = (D_FF - (nf - 1) * blk) // tf
    assert (nf - 1) * blk + n_sub_last * tf == D_FF
    emit_next = g_next is not None
    row = pl.BlockSpec((tm, d), lambda i, f: (i, 0))
    vec = pl.BlockSpec((1, d), lambda i, f: (0, 0))
    in_specs = [row, vec,
                pl.BlockSpec((d, blk), lambda i, f: (0, f)),
                pl.BlockSpec((d, blk), lambda i, f: (0, f)),
                pl.BlockSpec((blk, d), lambda i, f: (f, 0)),
                vec]
    args = [x, g_pre, wg, wu, wd, g_post]
    out_shape = [jax.ShapeDtypeStruct((t, d), F32)]
    out_specs = [row]
    if emit_next:
        in_specs.append(vec)
        args.append(g_next)
        out_shape.append(jax.ShapeDtypeStruct((t, d), BF16))
        out_specs.append(row)
    res = pl.pallas_call(
        functools.partial(_ffn_body, nf=nf, tf=tf, n_sub_last=n_sub_last, emit_next=emit_next),
        grid=(t // tm, nf),
        in_specs=in_specs,
        out_specs=out_specs,
        out_shape=out_shape,
        scratch_shapes=[pltpu.VMEM((tm, d), BF16)],
        compiler_params=pltpu.CompilerParams(
            dimension_semantics=("parallel", "arbitrary"), vmem_limit_bytes=VMEM_LIMIT),
        name="ffn_next" if emit_next else "ffn",
    )(*args)
    return res if emit_next else res[0]


def _proj_rope_body(h_ref, w_ref, cos_ref, sin_ref, o_ref, *, head_scales):
    h = h_ref[...]
    cos = cos_ref[...]
    sin = sin_ref[...]
    heads_per_chunk = MXU_N // HEAD_DIM
    for c in range(len(head_scales) // heads_per_chunk):
        acc = jnp.dot(h, w_ref[:, c * MXU_N:(c + 1) * MXU_N], preferred_element_type=F32)
        for k in range(heads_per_chunk):
            hd = c * heads_per_chunk + k
            a = acc[:, k * HEAD_DIM:(k + 1) * HEAD_DIM]
            r = a * cos + pltpu.roll(a, HEAD_DIM // 2, 1) * sin
            if head_scales[hd] != 1.0:
                r = r * head_scales[hd]
            o_ref[:, hd * HEAD_DIM:(hd + 1) * HEAD_DIM] = r.astype(BF16)


def _proj_plain_body(h_ref, w_ref, o_ref, *, chunk):
    h = h_ref[...]
    for c in range(w_ref.shape[1] // chunk):
        acc = jnp.dot(h, w_ref[:, c * chunk:(c + 1) * chunk], preferred_element_type=F32)
        o_ref[:, c * chunk:(c + 1) * chunk] = acc.astype(BF16)


def _proj_qkv_body(h_ref, wr_ref, wv_ref, cos_ref, sin_ref, or_ref, ov_ref, *, head_scales, chunk):
    _proj_rope_body(h_ref, wr_ref, cos_ref, sin_ref, or_ref, head_scales=head_scales)
    _proj_plain_body(h_ref, wv_ref, ov_ref, chunk=chunk)


def _proj_qkv(h, w_rope, w_v, cos2, sin2, head_scales, *, tm):
    t, d = h.shape
    ns = cos2.shape[0] // tm
    rows = lambda n: pl.BlockSpec((tm, n), lambda i: (i, 0))
    tab = pl.BlockSpec((tm, HEAD_DIM), lambda i: (i % ns, 0))
    return pl.pallas_call(
        functools.partial(_proj_qkv_body, head_scales=head_scales, chunk=MXU_N),
        grid=(t // tm,),
        in_specs=[rows(d), _resident(w_rope.shape), _resident(w_v.shape), tab, tab],
        out_specs=[rows(ROPE_COLS), rows(V_COLS)],
        out_shape=[jax.ShapeDtypeStruct((t, ROPE_COLS), BF16),
                   jax.ShapeDtypeStruct((t, V_COLS), BF16)],
        compiler_params=pltpu.CompilerParams(
            dimension_semantics=("parallel",), vmem_limit_bytes=VMEM_LIMIT),
        name="proj_qkv",
    )(h, w_rope, w_v, cos2, sin2)


def _proj_gate_body(h_ref, w_ref, b_ref, o_ref, *, chunk):
    h = h_ref[...]
    for c in range(w_ref.shape[1] // chunk):
        acc = jnp.dot(h, w_ref[:, c * chunk:(c + 1) * chunk], preferred_element_type=F32)
        z = acc + b_ref[:, c * chunk:(c + 1) * chunk]
        o_ref[:, c * chunk:(c + 1) * chunk] = (1.0 / (1.0 + jnp.exp(-z))).astype(BF16)


def _proj_call(body, h, w, extra, extra_specs, n_out, name, *, tm):
    t, d = h.shape
    return pl.pallas_call(
        body,
        grid=(t // tm,),
        in_specs=[pl.BlockSpec((tm, d), lambda i: (i, 0)), _resident(w.shape)] + extra_specs,
        out_specs=pl.BlockSpec((tm, n_out), lambda i: (i, 0)),
        out_shape=jax.ShapeDtypeStruct((t, n_out), BF16),
        compiler_params=pltpu.CompilerParams(
            dimension_semantics=("parallel",), vmem_limit_bytes=VMEM_LIMIT),
        name=name,
    )(h, w, *extra)


def _attn_a_body(sink_ref, q_ref, kp_ref, kc_ref, kn_ref, vp_ref, vc_ref, vn_ref, o_ref, *, tq, nq):
    i = pl.program_id(1)
    nk = tq + 2 * WINDOW
    row = lax.broadcasted_iota(jnp.int32, (tq, nk), 0)
    col = lax.broadcasted_iota(jnp.int32, (tq, nk), 1)
    rel = col - WINDOW - row
    valid = (rel >= -WINDOW) & (rel <= WINDOW)
    valid = valid & ((col >= WINDOW) | (i > 0))
    valid = valid & ((col < tq + WINDOW) | (i < nq - 1))
    k_all = jnp.concatenate([kp_ref[0], kc_ref[0], kn_ref[0]], axis=0)
    v_all = jnp.concatenate([vp_ref[0], vc_ref[0], vn_ref[0]], axis=0)
    for g in range(A_KV_HEADS):
        k = k_all[:, g * HEAD_DIM:(g + 1) * HEAD_DIM]
        v = v_all[:, g * HEAD_DIM:(g + 1) * HEAD_DIM]
        for r in range(A_REP):
            hd = g * A_REP + r
            q = q_ref[0, :, hd * HEAD_DIM:(hd + 1) * HEAD_DIM]
            s = lax.dot_general(q, k, (((1,), (1,)), ((), ())), preferred_element_type=F32)
            s = jnp.where(valid, s, NEG_INF)
            sk = sink_ref[hd] * LOG2E
            m = jnp.maximum(jnp.max(s, axis=-1, keepdims=True), sk)
            e = jnp.exp2(s - m)
            den = jnp.sum(e, axis=-1, keepdims=True) + jnp.exp2(sk - m)
            o = jnp.dot(e.astype(BF16), v, preferred_element_type=F32) / den
            o_ref[0, :, hd * HEAD_DIM:(hd + 1) * HEAD_DIM] = o.astype(BF16)


def _attn_a(rope3, v3, sink, *, tq=512):
    b, s, _ = rope3.shape
    nq = s // tq
    nblk = s // WINDOW
    per = tq // WINDOW
    ka_blk = A_Q // A_KV
    prev_map = lambda bi, i: (bi, jnp.maximum(i * per - 1, 0), ka_blk)
    cur_map = lambda bi, i: (bi, i, ka_blk)
    next_map = lambda bi, i: (bi, jnp.minimum((i + 1) * per, nblk - 1), ka_blk)
    vprev_map = lambda bi, i: (bi, jnp.maximum(i * per - 1, 0), 0)
    vcur_map = lambda bi, i: (bi, i, 0)
    vnext_map = lambda bi, i: (bi, jnp.minimum((i + 1) * per, nblk - 1), 0)
    return pl.pallas_call(
        functools.partial(_attn_a_body, tq=tq, nq=nq),
        grid=(b, nq),
        in_specs=[
            pl.BlockSpec(memory_space=pltpu.SMEM),
            pl.BlockSpec((1, tq, A_Q), lambda bi, i: (bi, i, 0)),
            pl.BlockSpec((1, WINDOW, A_KV), prev_map),
            pl.BlockSpec((1, tq, A_KV), cur_map),
            pl.BlockSpec((1, WINDOW, A_KV), next_map),
            pl.BlockSpec((1, WINDOW, A_KV), vprev_map),
            pl.BlockSpec((1, tq, A_KV), vcur_map),
            pl.BlockSpec((1, WINDOW, A_KV), vnext_map),
        ],
        out_specs=pl.BlockSpec((1, tq, A_Q), lambda bi, i: (bi, i, 0)),
        out_shape=jax.ShapeDtypeStruct((b, s, A_Q), BF16),
        compiler_params=pltpu.CompilerParams(
            dimension_semantics=("parallel", "parallel"), vmem_limit_bytes=VMEM_LIMIT),
        name="attn_a",
    )(sink, rope3, rope3, rope3, rope3, v3, v3, v3)


def _attn_b_body(q_ref, k_ref, v_ref, lam_ref, g_ref, o_ref,
                 s_scr, mx_scr, p_scr, a_scr, m_scr, l_scr, acc_scr, kn_scr, *, tk, nk, lam_init):
    reps = tk // LANES
    tq = q_ref.shape[1]

    def key_chunk(c):
        return k_ref[0, pl.ds(pl.multiple_of(c * tk, tk), tk), :]

    def val_chunk(c):
        return v_ref[0, pl.ds(pl.multiple_of(c * tk, tk), tk), :]

    def qk(c, comp):
        q = q_ref[0, :, comp * HEAD_DIM:(comp + 1) * HEAD_DIM]
        k = key_chunk(c)[:, comp * HEAD_DIM:(comp + 1) * HEAD_DIM]
        return lax.dot_general(q, k, (((1,), (1,)), ((), ())), preferred_element_type=F32)

    def lane_fold(x, op):
        y = x[:, :LANES]
        for j in range(1, reps):
            y = op(y, x[:, j * LANES:(j + 1) * LANES])
        return y

    @pl.when(pl.program_id(2) == 0)
    def _():
        def chunk_max(c, carry):
            kc = key_chunk(c).astype(F32)
            sq = kc * kc
            out = []
            for comp in range(2):
                n2 = jnp.sum(sq[:, comp * HEAD_DIM:(comp + 1) * HEAD_DIM], axis=-1, keepdims=True)
                out.append(jnp.maximum(carry[comp], jnp.max(n2, axis=0, keepdims=True)))
            return tuple(out)
        zero = jnp.zeros((1, 1), F32)
        n2max = lax.fori_loop(0, nk, chunk_max, (zero, zero))
        for comp in range(2):
            kn_scr[comp] = jnp.broadcast_to(jnp.sqrt(n2max[comp]), kn_scr.shape[1:])

    def scores(c, slot):
        for comp in range(2):
            s = qk(c, comp)
            s_scr[slot, comp] = s
            mx_scr[slot, comp] = lane_fold(s, jnp.maximum)

    def softmax(slot):
        for comp in range(2):
            m_prev = m_scr[comp]
            m_new = jnp.maximum(m_prev, jnp.max(mx_scr[slot, comp], axis=-1, keepdims=True))
            a_scr[slot, comp] = jnp.exp2(m_prev - m_new)
            m_scr[comp] = m_new
            e = jnp.exp2(s_scr[slot, comp] - _lane_tile(m_new, reps))
            l_scr[comp] = a_scr[slot, comp] * l_scr[comp] + lane_fold(e, jnp.add)
            p_scr[slot, comp] = e.astype(BF16)

    def pv(c, slot):
        vc = val_chunk(c)
        for comp in range(2):
            acc_scr[comp] = _lane_tile(a_scr[slot, comp], 2) * acc_scr[comp] + jnp.dot(
                p_scr[slot, comp], vc, preferred_element_type=F32)

    def steady_pair(i):
        pv(i - 1, 0)
        scores(i + 1, 0)
        softmax(1)
        pv(i, 1)
        scores(i + 2, 1)
        softmax(0)

    def steady_quad(j, carry):
        steady_pair(4 * j + 1)
        steady_pair(4 * j + 3)
        return carry

    def probs(c, slot, s_of):
        for comp in range(2):
            e = jnp.exp2(s_of(comp) - _lane_tile(m_scr[comp], reps))
            l_scr[comp] += lane_fold(e, jnp.add)
            p_scr[slot, comp] = e.astype(BF16)

    def pv_plain(c, slot):
        vc = val_chunk(c)
        for comp in range(2):
            acc_scr[comp] += jnp.dot(p_scr[slot, comp], vc, preferred_element_type=F32)

    def fixed_steady():
        for c in range(2, nk):
            pv_plain(c - 2, c % 2)
            probs(c, c % 2, lambda comp: qk(c, comp))

    l_scr[...] = jnp.zeros_like(l_scr)
    acc_scr[...] = jnp.zeros_like(acc_scr)
    scores(0, 0)
    scores(1, 1)
    span = jnp.zeros((1, 1), F32)
    for comp in range(2):
        qf = q_ref[0, :, comp * HEAD_DIM:(comp + 1) * HEAD_DIM].astype(F32)
        q_norm = jnp.sqrt(jnp.sum(qf * qf, axis=-1, keepdims=True))
        bound = q_norm * kn_scr[comp][:1, :1]
        first_max = jnp.max(mx_scr[0, comp], axis=-1, keepdims=True)
        span = jnp.maximum(span, jnp.max(bound - first_max, axis=0, keepdims=True))
        m_scr[comp] = jnp.broadcast_to(first_max, (tq, LANES))
    use_fixed = span[0, 0] <= FIXED_SHIFT_SPAN
    probs(0, 0, lambda comp: s_scr[0, comp])

    @pl.when(use_fixed)
    def _():
        probs(1, 1, lambda comp: s_scr[1, comp])
        fixed_steady()
        pv_plain(nk - 2, 0)
        pv_plain(nk - 1, 1)

    @pl.when(jnp.logical_not(use_fixed))
    def _():
        m_scr[...] = jnp.full_like(m_scr, NEG_INF)
        l_scr[...] = jnp.zeros_like(l_scr)
        softmax(0)
        n_pairs = (nk - 2) // 2
        lax.fori_loop(0, n_pairs // 2, steady_quad, 0)
        if n_pairs % 2:
            steady_pair(nk - 3)
        softmax(1)
        pv(nk - 2, 0)
        pv(nk - 1, 1)

    lam_v = lam_ref[...]
    lam = (jnp.exp(jnp.sum(lam_v[0:1] * lam_v[1:2], axis=-1, keepdims=True))
           - jnp.exp(jnp.sum(lam_v[2:3] * lam_v[3:4], axis=-1, keepdims=True))
           + lam_init)
    l0 = jnp.sum(l_scr[0], axis=-1, keepdims=True)
    l1 = jnp.sum(l_scr[1], axis=-1, keepdims=True)
    o = acc_scr[0] / l0 - lam * (acc_scr[1] / l1)
    o_ref[0] = (_rms(o, g_ref[...]) * (1.0 - lam_init)).astype(BF16)


def _attn_b(rope3, v3, lam_vecs, subln_g, lam_init, *, tq=512, tk=512):
    b, s, _ = rope3.shape
    nk = s // tk
    assert nk % 2 == 0 and nk >= 4
    w = 2 * HEAD_DIM
    qb_blk = (A_Q + A_KV) // w
    kb_blk = qb_blk + B_HEADS
    vb_blk = A_KV // w
    return pl.pallas_call(
        functools.partial(_attn_b_body, tk=tk, nk=nk, lam_init=lam_init),
        grid=(b, B_HEADS, s // tq),
        in_specs=[
            pl.BlockSpec((1, tq, w), lambda bi, h, i: (bi, i, qb_blk + h)),
            pl.BlockSpec((1, s, w), lambda bi, h, i: (bi, 0, kb_blk + h)),
            pl.BlockSpec((1, s, w), lambda bi, h, i: (bi, 0, vb_blk + h)),
            pl.BlockSpec((4, HEAD_DIM), lambda bi, h, i: (0, 0)),
            pl.BlockSpec((1, w), lambda bi, h, i: (0, 0)),
        ],
        out_specs=pl.BlockSpec((1, tq, w), lambda bi, h, i: (bi, i, h)),
        out_shape=jax.ShapeDtypeStruct((b, s, B_V), BF16),
        scratch_shapes=[pltpu.VMEM((2, 2, tq, tk), F32),
                        pltpu.VMEM((2, 2, tq, LANES), F32),
                        pltpu.VMEM((2, 2, tq, tk), BF16),
                        pltpu.VMEM((2, 2, tq, LANES), F32),
                        pltpu.VMEM((2, tq, LANES), F32),
                        pltpu.VMEM((2, tq, LANES), F32),
                        pltpu.VMEM((2, tq, w), F32),
                        pltpu.VMEM((2, 8, LANES), F32)],
        compiler_params=pltpu.CompilerParams(
            dimension_semantics=("parallel", "parallel", "arbitrary"),
            vmem_limit_bytes=VMEM_LIMIT),
        name="attn_b",
    )(rope3, rope3, v3, lam_vecs, subln_g)


def _merge_body(oa_ref, ob_ref, gate_ref, x_ref, wa_ref, wb_ref, wo_ref, g_ref, o_ref):
    pa = jnp.dot(oa_ref[...], wa_ref[...], preferred_element_type=F32)
    pb = jnp.dot(ob_ref[...], wb_ref[...], preferred_element_type=F32)
    ga = gate_ref[:, :D_MODEL].astype(F32)
    gb = gate_ref[:, D_MODEL:].astype(F32)
    merged = (ga * pa + gb * pb).astype(BF16)
    y = jnp.dot(merged, wo_ref[...], preferred_element_type=F32)
    o_ref[...] = x_ref[...] + _rms(y, g_ref[...])


def _merge(oa, ob, gates, x1, wa, wb, wo, g_post, *, tm=512):
    t, d = x1.shape
    rows = lambda n: pl.BlockSpec((tm, n), lambda i: (i, 0))
    return pl.pallas_call(
        _merge_body,
        grid=(t // tm,),
        in_specs=[rows(A_Q), rows(B_V), rows(GATE_COLS), rows(d),
                  _resident(wa.shape), _resident(wb.shape), _resident(wo.shape),
                  pl.BlockSpec((1, d), lambda i: (0, 0))],
        out_specs=rows(d),
        out_shape=jax.ShapeDtypeStruct((t, d), F32),
        compiler_params=pltpu.CompilerParams(
            dimension_semantics=("parallel",), vmem_limit_bytes=VMEM_LIMIT),
        name="merge",
    )(oa, ob, gates, x1, wa, wb, wo, g_post)


def _rope_tables(seq):
    pos = jnp.arange(seq, dtype=F32)
    inv = ROPE_THETA ** (-jnp.arange(0, HEAD_DIM, 2, dtype=F32) / HEAD_DIM)
    ang = pos[:, None] * inv[None, :]
    cos, sin = jnp.cos(ang), jnp.sin(ang)
    return jnp.concatenate([cos, cos], axis=-1), jnp.concatenate([-sin, sin], axis=-1)


def kernel(x, ffn1_pre_g, ffn1_w_gate, ffn1_w_up, ffn1_w_down, ffn1_post_g, mix_pre_g, w_in, gate_bias, sink_logit, lambda_q1, lambda_k1, lambda_q2, lambda_k2, subln_g, w_proj_a, w_proj_b, w_out, mix_post_g, ffn2_pre_g, ffn2_w_gate, ffn2_w_up, ffn2_w_down, ffn2_post_g):
    b, s, d = x.shape
    t = b * s
    depth = ffn1_pre_g.shape[0]
    cos2, sin2 = _rope_tables(s)
    xt = x.reshape(t, d)
    q_scaled = (QK_SCALE,) * A_Q_HEADS + (1.0,) * A_KV_HEADS + (QK_SCALE,) * (2 * B_HEADS) + (1.0,) * (2 * B_HEADS)
    tm_proj = 1024
    for l in range(depth):
        lam_init = 0.8 - 0.6 * math.exp(-0.3 * l)
        row = lambda p: p[l:l + 1]
        bf = lambda p: p[l].astype(BF16)

        xt, h = _ffn(xt, row(ffn1_pre_g), bf(ffn1_w_gate), bf(ffn1_w_up), bf(ffn1_w_down),
                     row(ffn1_post_g), row(mix_pre_g))

        w = w_in[l]
        qa_end, ka_end, va_end = A_Q, A_Q + A_KV, A_Q + 2 * A_KV
        qb_end = va_end + B_QK
        kb_end = qb_end + B_QK
        vb_end = kb_end + B_V
        w_rope = jnp.concatenate([w[:, :ka_end], w[:, va_end:kb_end]], axis=1).astype(BF16)
        w_v = jnp.concatenate([w[:, ka_end:va_end], w[:, kb_end:vb_end]], axis=1).astype(BF16)
        w_gate = w[:, vb_end:].astype(BF16)

        rope, vals = _proj_qkv(h, w_rope, w_v, cos2, sin2, q_scaled, tm=tm_proj)
        gates = _proj_call(functools.partial(_proj_gate_body, chunk=2 * MXU_N),
                           h, w_gate, [gate_bias[l:l + 1]], [_resident((1, GATE_COLS))],
                           GATE_COLS, "proj_gate", tm=tm_proj)

        rope3 = rope.reshape(b, s, ROPE_COLS)
        v3 = vals.reshape(b, s, V_COLS)
        out_a = _attn_a(rope3, v3, sink_logit[l])
        lam_vecs = jnp.stack([lambda_q1[l], lambda_k1[l], lambda_q2[l], lambda_k2[l]]).astype(F32)
        out_b = _attn_b(rope3, v3, lam_vecs, row(subln_g), lam_init)

        xt = _merge(out_a.reshape(t, A_Q), out_b.reshape(t, B_V), gates, xt,
                    bf(w_proj_a), bf(w_proj_b), bf(w_out), row(mix_post_g))

        xt = _ffn(xt, row(ffn2_pre_g), bf(ffn2_w_gate), bf(ffn2_w_up), bf(ffn2_w_down),
                  row(ffn2_post_g))
    return xt.reshape(b, s, d)
```
